```python
import math
import jax, jax.numpy as jnp
from jax import lax
import numpy as np

D_MODEL = 1024
BATCH = 32
SEQ = 2048
DEPTH = 2

N_MIXERS = 2
RMS_EPS = 1e-6

MLA_HEADS = 8
QK_NOPE_DIM = 128
QK_ROPE_DIM = 64
V_HEAD_DIM = 128
Q_LORA_RANK = 384
KV_LORA_RANK = 256
ROPE_THETA = 10000.0
Q_BLOCK = 128
MLA_WIDTH = MLA_HEADS * V_HEAD_DIM
MLA_QK_DIM = QK_NOPE_DIM + QK_ROPE_DIM
MLA_IN_WIDTH = Q_LORA_RANK + KV_LORA_RANK + QK_ROPE_DIM + MLA_WIDTH

HYENA_WIDTH = D_MODEL
HYENA_ORDER = 2
SHORT_CONV = 3
POS_EMB_DIM = 33
POS_BANDS = (POS_EMB_DIM - 1) // 2
FILTER_HIDDEN = 64
N_DIRS = 2
DECAY_TARGET = 1e-2
FAST_DECAY_PCT = 0.3
SLOW_DECAY_PCT = 1.5
MIN_DECAY = math.log(DECAY_TARGET) / SLOW_DECAY_PCT
MAX_DECAY = math.log(DECAY_TARGET) / FAST_DECAY_PCT
FILTER_OUT_GAIN = 0.05
HYENA_IN_WIDTH = (HYENA_ORDER + 1) * HYENA_WIDTH + HYENA_WIDTH

kernel_name = "mla_hyena_interleaved_encoder"


def rms_norm(x, g):
    xf = x.astype(jnp.float32)
    y = xf * lax.rsqrt(jnp.mean(xf * xf, axis=-1, keepdims=True) + RMS_EPS)
    return (y * g.astype(jnp.float32)).astype(x.dtype)


def rope_tables(L):
    inv = 1.0 / (ROPE_THETA ** (jnp.arange(0, QK_ROPE_DIM, 2, dtype=jnp.float32) / QK_ROPE_DIM))
    ang = jnp.arange(L, dtype=jnp.float32)[:, None] * inv[None, :]
    return jnp.cos(ang), jnp.sin(ang)


def apply_rope(x, cos, sin):
    xf = x.astype(jnp.float32)
    x1, x2 = xf[..., :QK_ROPE_DIM // 2], xf[..., QK_ROPE_DIM // 2:]
    out = jnp.concatenate([x1 * cos - x2 * sin, x1 * sin + x2 * cos], axis=-1)
    return out.astype(x.dtype)


def mla_mixer(h, w_in, q_norm, w_uq, kv_norm, w_ukv, w_out):
    B, L, _ = h.shape
    proj = h @ w_in
    s1 = Q_LORA_RANK
    s2 = s1 + KV_LORA_RANK
    s3 = s2 + QK_ROPE_DIM
    c_q, c_kv, k_pe, gate = proj[..., :s1], proj[..., s1:s2], proj[..., s2:s3], proj[..., s3:]
    q = (rms_norm(c_q, q_norm) @ w_uq).reshape(B, L, MLA_HEADS, MLA_QK_DIM)
    q_nope, q_pe = q[..., :QK_NOPE_DIM], q[..., QK_NOPE_DIM:]
    kv = (rms_norm(c_kv, kv_norm) @ w_ukv).reshape(B, L, MLA_HEADS, QK_NOPE_DIM + V_HEAD_DIM)
    k_nope, v = kv[..., :QK_NOPE_DIM], kv[..., QK_NOPE_DIM:]
    cos, sin = rope_tables(L)
    q_pe = apply_rope(q_pe, cos[:, None, :], sin[:, None, :])
    k_pe = apply_rope(k_pe, cos, sin)
    scale = MLA_QK_DIM ** -0.5
    n_blk = L // Q_BLOCK

    def to_blocks(t):
        return jnp.moveaxis(t.reshape(B, n_blk, Q_BLOCK, *t.shape[2:]), 1, 0)

    def attend(blk):
        qn, qp = blk
        s = (jnp.einsum('bqhd,bkhd->bhqk', qn, k_nope)
             + jnp.einsum('bqhr,bkr->bhqk', qp, k_pe))
        p = jax.nn.softmax(s.astype(jnp.float32) * scale, axis=-1).astype(v.dtype)
        return jnp.einsum('bhqk,bkhd->bqhd', p, v)

    o = lax.map(attend, (to_blocks(q_nope), to_blocks(q_pe)))
    o = jnp.moveaxis(o, 0, 1).reshape(B, L, MLA_WIDTH)
    return (o * jax.nn.silu(gate)) @ w_out


def hyena_filters(L, w1, b1, w2, b2, w3, b3, w4, freq):
    f32 = jnp.float32
    t = jnp.linspace(0.0, 1.0, L, dtype=f32)[:, None]
    w = 2.0 * math.pi * jnp.arange(L, dtype=f32) / L
    bands = jnp.linspace(1e-4, POS_BANDS - 1, POS_BANDS, dtype=f32)
    fw = w[:, None] * bands[None, :]
    z = jnp.concatenate([t, jnp.cos(fw), -jnp.sin(fw)], axis=-1)
    fr = freq.astype(f32)
    a = jnp.sin(fr * (z @ w1.astype(f32) + b1.astype(f32)))
    a = jnp.sin(fr * (a @ w2.astype(f32) + b2.astype(f32)))
    a = jnp.sin(fr * (a @ w3.astype(f32) + b3.astype(f32)))
    hf = (a @ w4.astype(f32)).reshape(L, HYENA_ORDER, N_DIRS, HYENA_WIDTH)
    deltas = jnp.abs(jnp.linspace(MIN_DECAY, MAX_DECAY, HYENA_WIDTH, dtype=f32))
    decay = jnp.exp(-t * deltas[None, :])
    return hf * decay[:, None, None, :]


def two_sided_kernel(h_fwd, h_bwd):
    return jnp.concatenate([h_fwd, jnp.zeros_like(h_fwd[:1]), h_bwd[:0:-1]], axis=0)


def bidir_long_conv(u, k, bias):
    L = u.shape[1]
    uf = u.astype(jnp.float32)
    spec = jnp.fft.rfft(uf, n=2 * L, axis=1) * jnp.fft.rfft(k, n=2 * L, axis=0)[None]
    y = jnp.fft.irfft(spec, n=2 * L, axis=1)[:, :L]
    return (y + uf * bias.astype(jnp.float32)).astype(u.dtype)


def hyena_mixer(h, w_in, conv_w, conv_b, filt_w1, filt_b1, filt_w2, filt_b2,
                filt_w3, filt_b3, filt_w4, filt_freq, filt_bias, w_out):
    B, L, _ = h.shape
    proj = h @ w_in
    u, gate = proj[..., :(HYENA_ORDER + 1) * HYENA_WIDTH], proj[..., (HYENA_ORDER + 1) * HYENA_WIDTH:]
    pad = SHORT_CONV // 2
    up = jnp.pad(u, ((0, 0), (pad, pad), (0, 0)))
    u = conv_b + sum(up[:, j:j + L] * conv_w[j] for j in range(SHORT_CONV))
    x1 = u[..., :HYENA_WIDTH]
    x2 = u[..., HYENA_WIDTH:2 * HYENA_WIDTH]
    v = u[..., 2 * HYENA_WIDTH:]
    filt = hyena_filters(L, filt_w1, filt_b1, filt_w2, filt_b2, filt_w3, filt_b3, filt_w4, filt_freq)
    z = v
    for n, g_n in enumerate((x1, x2)):
        k = two_sided_kernel(filt[:, n, 0], filt[:, n, 1])
        z = g_n * bidir_long_conv(z, k, filt_bias[n])
    return (z * jax.nn.silu(gate)) @ w_out


def setup_inputs(seed: int = 0) -> dict:
    key = jax.random.key(seed)
    ks = iter(jax.random.split(key, 32))
    f32 = jnp.float32

    def nrm(shape, scale):
        return jax.random.normal(next(ks), shape, f32) * scale

    def gain(n):
        return jnp.ones((n,), f32) + nrm((n,), 0.01)

    d = D_MODEL
    return {
        "x": nrm((BATCH, SEQ, d), 1.0),
        "l0_norm": gain(d),
        "l0_w_in": nrm((d, MLA_IN_WIDTH), d ** -0.5),
        "l0_q_norm": gain(Q_LORA_RANK),
        "l0_w_uq": nrm((Q_LORA_RANK, MLA_HEADS * MLA_QK_DIM), Q_LORA_RANK ** -0.5),
        "l0_kv_norm": gain(KV_LORA_RANK),
        "l0_w_ukv": nrm((KV_LORA_RANK, MLA_HEADS * (QK_NOPE_DIM + V_HEAD_DIM)), KV_LORA_RANK ** -0.5),
        "l0_w_out": nrm((MLA_WIDTH, d), MLA_WIDTH ** -0.5),
        "l1_norm": gain(d),
        "l1_w_in": nrm((d, HYENA_IN_WIDTH), d ** -0.5),
        "l1_conv_w": nrm((SHORT_CONV, (HYENA_ORDER + 1) * HYENA_WIDTH), SHORT_CONV ** -0.5),
        "l1_conv_b": nrm(((HYENA_ORDER + 1) * HYENA_WIDTH,), 0.02),
        "l1_filt_w1": nrm((POS_EMB_DIM, FILTER_HIDDEN), POS_EMB_DIM ** -0.5),
        "l1_filt_b1": nrm((FILTER_HIDDEN,), 0.1),
        "l1_filt_w2": nrm((FILTER_HIDDEN, FILTER_HIDDEN), FILTER_HIDDEN ** -0.5),
        "l1_filt_b2": nrm((FILTER_HIDDEN,), 0.1),
        "l1_filt_w3": nrm((FILTER_HIDDEN, FILTER_HIDDEN), FILTER_HIDDEN ** -0.5),
        "l1_filt_b3": nrm((FILTER_HIDDEN,), 0.1),
        "l1_filt_w4": nrm((FILTER_HIDDEN, HYENA_ORDER * N_DIRS * HYENA_WIDTH), FILTER_OUT_GAIN * FILTER_HIDDEN ** -0.5),
        "l1_filt_freq": jnp.ones((FILTER_HIDDEN,), f32) + nrm((FILTER_HIDDEN,), 0.1),
        "l1_filt_bias": nrm((HYENA_ORDER, HYENA_WIDTH), 0.2),
        "l1_w_out": nrm((HYENA_WIDTH, d), HYENA_WIDTH ** -0.5),
        "final_norm": gain(d),
    }


def reference(x, l0_norm, l0_w_in, l0_q_norm, l0_w_uq, l0_kv_norm, l0_w_ukv, l0_w_out,
              l1_norm, l1_w_in, l1_conv_w, l1_conv_b, l1_filt_w1, l1_filt_b1, l1_filt_w2,
              l1_filt_b2, l1_filt_w3, l1_filt_b3, l1_filt_w4, l1_filt_freq, l1_filt_bias,
              l1_w_out, final_norm):
    mixers = (mla_mixer, hyena_mixer)
    layers = (
        (l0_norm, (l0_w_in, l0_q_norm, l0_w_uq, l0_kv_norm, l0_w_ukv, l0_w_out)),
        (l1_norm, (l1_w_in, l1_conv_w, l1_conv_b, l1_filt_w1, l1_filt_b1, l1_filt_w2, l1_filt_b2,
                   l1_filt_w3, l1_filt_b3, l1_filt_w4, l1_filt_freq, l1_filt_bias, l1_w_out)),
    )
    h = x
    for i in range(DEPTH):
        g, params = layers[i]
        h = h + mixers[i % N_MIXERS](rms_norm(h, g), *params)
    return rms_norm(h, final_norm)
```

```python
import functools
import math

import jax
import jax.numpy as jnp
import numpy as np
from jax import lax
from jax.experimental import pallas as pl
from jax.experimental.pallas import tpu as pltpu

RMS_EPS = 1e-6
HEADS = 8
NOPE = 128
ROPE = 64
VDIM = 128
Q_RANK = 384
KV_RANK = 256
ROPE_THETA = 10000.0
QK_DIM = NOPE + ROPE

POS_EMB = 33
POS_BANDS = (POS_EMB - 1) // 2
FILT_HIDDEN = 64
MIN_DECAY = math.log(1e-2) / 1.5
MAX_DECAY = math.log(1e-2) / 0.3

LANES = 128
SUBLANES = 8
MXU_N = 256
VMEM_LIMIT = 56 * 1024 * 1024

F32 = jnp.float32
BF16 = jnp.bfloat16


def _rms(x, g):
    return x * lax.rsqrt(jnp.mean(x * x, axis=-1, keepdims=True) + RMS_EPS) * g


def _dot(a, b):
    return jnp.dot(a, b, preferred_element_type=F32)


def _params(sem):
    return pltpu.CompilerParams(dimension_semantics=sem, vmem_limit_bytes=VMEM_LIMIT)


def _mla_in_kernel(x_ref, g0_ref, win_ref, qn_ref, wuq_ref, wuqs_ref, kvn_ref, wukv_ref,
                   cos_ref, sin_ref, q_ref, k_ref, v_ref, g_ref):
    scale = QK_DIM ** -0.5
    xn = _rms(x_ref[...], g0_ref[...]).astype(BF16)
    proj = _dot(xn, win_ref[...])
    c_q = proj[:, :Q_RANK]
    c_kv = proj[:, Q_RANK:Q_RANK + KV_RANK]
    o = Q_RANK + KV_RANK
    kpe = proj[:, o:o + LANES]
    kpe_sw = proj[:, o + LANES:o + 2 * LANES]
    gate = proj[:, o + 2 * LANES:]
    cos = cos_ref[...]
    sin = sin_ref[...]

    cqn = _rms(c_q, qn_ref[...]).astype(BF16)
    qf = _dot(cqn, wuq_ref[...])
    qs = _dot(cqn, wuqs_ref[...])
    for h in range(HEADS):
        nope = qf[:, 2 * LANES * h:2 * LANES * h + LANES]
        pe = qf[:, 2 * LANES * h + LANES:2 * LANES * (h + 1)]
        pe = pe * cos + qs[:, LANES * h:LANES * (h + 1)] * sin
        q_ref[:, 2 * LANES * h:2 * LANES * h + LANES] = (nope * scale).astype(BF16)
        q_ref[:, 2 * LANES * h + LANES:2 * LANES * (h + 1)] = (pe * scale).astype(BF16)

    ckvn = _rms(c_kv, kvn_ref[...]).astype(BF16)
    kv = _dot(ckvn, wukv_ref[...])
    kpe_r = (kpe * cos + kpe_sw * sin).astype(BF16)
    for h in range(HEADS):
        k_ref[:, 2 * LANES * h:2 * LANES * h + LANES] = kv[:, LANES * h:LANES * (h + 1)].astype(BF16)
        k_ref[:, 2 * LANES * h + LANES:2 * LANES * (h + 1)] = kpe_r
    v_ref[...] = kv[:, HEADS * NOPE:].astype(BF16)
    g_ref[...] = (gate * jax.nn.sigmoid(gate)).astype(BF16)


def _mla_in(x2d, g0, win, qn, wuq, wuqs, kvn, wukv, cos_t, sin_t, seq, tm):
    n, d = x2d.shape
    nt = seq // tm
    full = lambda a: pl.BlockSpec(a.shape, lambda i: (0,) * a.ndim)
    return pl.pallas_call(
        _mla_in_kernel,
        grid=(n // tm,),
        in_specs=[pl.BlockSpec((tm, d), lambda i: (i, 0)), full(g0), full(win), full(qn), full(wuq),
                  full(wuqs), full(kvn), full(wukv),
                  pl.BlockSpec((tm, LANES), lambda i: (i % nt, 0)),
                  pl.BlockSpec((tm, LANES), lambda i: (i % nt, 0))],
        out_specs=[pl.BlockSpec((tm, HEADS * 2 * LANES), lambda i: (i, 0)),
                   pl.BlockSpec((tm, HEADS * 2 * LANES), lambda i: (i, 0)),
                   pl.BlockSpec((tm, HEADS * VDIM), lambda i: (i, 0)),
                   pl.BlockSpec((tm, HEADS * VDIM), lambda i: (i, 0))],
        out_shape=[jax.ShapeDtypeStruct((n, HEADS * 2 * LANES), BF16),
                   jax.ShapeDtypeStruct((n, HEADS * 2 * LANES), BF16),
                   jax.ShapeDtypeStruct((n, HEADS * VDIM), BF16),
                   jax.ShapeDtypeStruct((n, HEADS * VDIM), BF16)],
        compiler_params=_params(("parallel",)),
        name="mla_in",
    )(x2d, g0, win, qn, wuq, wuqs, kvn, wukv, cos_t, sin_t)


def _attn_kernel(q_ref, k_ref, v_ref, g_ref, x_ref, wout_ref, g1_ref, h_ref, hn_ref, o_ref):
    for h in range(HEADS):
        qh = q_ref[0, :, 2 * LANES * h:2 * LANES * (h + 1)]
        kh = k_ref[0, :, 2 * LANES * h:2 * LANES * (h + 1)]
        s = lax.dot_general(qh, kh, (((1,), (1,)), ((), ())), preferred_element_type=F32)
        m = jnp.max(s, axis=-1, keepdims=True)
        p = jnp.exp(s - m)
        l = jnp.sum(p, axis=-1, keepdims=True)
        oh = _dot(p.astype(BF16), v_ref[0, :, VDIM * h:VDIM * (h + 1)])
        o_ref[:, VDIM * h:VDIM * (h + 1)] = oh / l
    og = (o_ref[...] * g_ref[0].astype(F32)).astype(BF16)
    h1 = x_ref[0] + _dot(og, wout_ref[...])
    h_ref[0] = h1
    hn_ref[0] = _rms(h1, g1_ref[...]).astype(BF16)


def _attention(q, k, v, g, x, wout, g1, tq):
    b, seq, d = x.shape
    qspec = lambda w: pl.BlockSpec((1, tq, w), lambda i, j: (i, j, 0))
    kspec = lambda w: pl.BlockSpec((1, seq, w), lambda i, j: (i, 0, 0))
    full = lambda a: pl.BlockSpec(a.shape, lambda i, j: (0,) * a.ndim)
    return pl.pallas_call(
        _attn_kernel,
        grid=(b, seq // tq),
        in_specs=[qspec(q.shape[-1]), kspec(k.shape[-1]), kspec(v.shape[-1]), qspec(g.shape[-1]),
                  qspec(d), full(wout), full(g1)],
        out_specs=[qspec(d), qspec(d)],
        out_shape=[jax.ShapeDtypeStruct((b, seq, d), F32), jax.ShapeDtypeStruct((b, seq, d), BF16)],
        scratch_shapes=[pltpu.VMEM((tq, HEADS * VDIM), F32)],
        compiler_params=_params(("parallel", "arbitrary")),
        name="mla_attn",
    )(q, k, v, g, x, wout, g1)


def _hyena_in_kernel(hn_ref, w_ref, u_ref, sg_ref):
    proj = _dot(hn_ref[...], w_ref[...])
    wu = u_ref.shape[-1]
    u_ref[...] = proj[:, :wu].astype(BF16)
    gate = proj[:, wu:]
    sg_ref[...] = (gate * jax.nn.sigmoid(gate)).astype(BF16)


def _hyena_in(hn2d, w, width, tm):
    n, d = hn2d.shape
    return pl.pallas_call(
        _hyena_in_kernel,
        grid=(n // tm,),
        in_specs=[pl.BlockSpec((tm, d), lambda i: (i, 0)), pl.BlockSpec(w.shape, lambda i: (0, 0))],
        out_specs=[pl.BlockSpec((tm, 3 * width), lambda i: (i, 0)),
                   pl.BlockSpec((tm, width), lambda i: (i, 0))],
        out_shape=[jax.ShapeDtypeStruct((n, 3 * width), BF16), jax.ShapeDtypeStruct((n, width), BF16)],
        compiler_params=_params(("parallel",)),
        name="hyena_in",
    )(hn2d, w)


def _filter_kernel(z_ref, w1_ref, b1_ref, w2_ref, b2_ref, w3_ref, b3_ref, fr_ref, w4_ref, dec_ref,
                   kf_ref, a_ref):
    seq = a_ref.shape[1] // 2
    hi = lax.Precision.HIGHEST

    @pl.when((pl.program_id(0) == 0) & (pl.program_id(1) == 0))
    def _():
        fr = fr_ref[...]
        a = jnp.sin(fr * (jnp.dot(w1_ref[...], z_ref[...], precision=hi, preferred_element_type=F32)
                          + b1_ref[...]))
        a = jnp.sin(fr * (jnp.dot(w2_ref[...], a, precision=hi, preferred_element_type=F32) + b2_ref[...]))
        a = jnp.sin(fr * (jnp.dot(w3_ref[...], a, precision=hi, preferred_element_type=F32) + b3_ref[...]))
        a_ref[...] = a

    bwd = jnp.dot(w4_ref[0, 1], a_ref[:, :seq], precision=hi, preferred_element_type=F32)
    fwd = jnp.dot(w4_ref[0, 0], a_ref[:, seq:], precision=hi, preferred_element_type=F32)
    kf_ref[0, :, :seq] = bwd * dec_ref[:, :seq]
    kf_ref[0, :, seq:] = fwd * dec_ref[:, seq:]


def _filters(zcat, w1t, b1, w2t, b2, w3t, b3, fr, w4t, deccat, tc):
    _, _, width, hid = w4t.shape
    two_l = zcat.shape[1]
    full = lambda a: pl.BlockSpec(a.shape, lambda n, i: (0,) * a.ndim)
    return pl.pallas_call(
        _filter_kernel,
        grid=(2, width // tc),
        in_specs=[full(zcat), full(w1t), full(b1), full(w2t), full(b2), full(w3t), full(b3), full(fr),
                  pl.BlockSpec((1, 2, tc, hid), lambda n, i: (n, 0, i, 0)),
                  pl.BlockSpec((tc, two_l), lambda n, i: (i, 0))],
        out_specs=pl.BlockSpec((1, tc, two_l), lambda n, i: (n, i, 0)),
        out_shape=jax.ShapeDtypeStruct((2, width, two_l), F32),
        scratch_shapes=[pltpu.VMEM((hid, two_l), F32)],
        compiler_params=_params(("arbitrary", "arbitrary")),
        name="hyena_filters",
    )(zcat, w1t, b1, w2t, b2, w3t, b3, fr, w4t, deccat)


def _conv_kernel(x1_ref, x2_ref, v_ref, kf1_ref, kf2_ref, par_ref, o_ref,
                 lhs_ref, rhs_ref, vc_ref, z1_ref, kfb_ref, *, ct):
    nj, nb = v_ref.shape[1], v_ref.shape[2]
    ni = nj // 2
    kdim = rhs_ref.shape[0]
    ng = kdim // LANES
    pad = 2 * ni - 2

    @pl.when(pl.program_id(0) == 0)
    def _():
        lhs_ref[...] = jnp.zeros(lhs_ref.shape, lhs_ref.dtype)

    lane = lax.broadcasted_iota(jnp.int32, (nb, LANES), 1)
    first_lane = lane == 0
    last_lane = lane == LANES - 1
    diff = (lax.broadcasted_iota(jnp.int32, (SUBLANES, LANES), 1)
            - lax.broadcasted_iota(jnp.int32, (SUBLANES, LANES), 0))
    zero_slab = jnp.zeros((nb, LANES), F32)

    def short_conv(ref, c, j, w0, w1, w2, cb):
        x0 = ref[c, j].astype(F32)
        xm = ref[c, j - 1].astype(F32) if j > 0 else zero_slab
        xp = ref[c, j + 1].astype(F32) if j < nj - 1 else zero_slab
        prev = pltpu.roll(jnp.where(last_lane, xm, x0), 1, axis=1)
        nxt = pltpu.roll(jnp.where(first_lane, xp, x0), LANES - 1, axis=1)
        return cb + w0 * prev + w1 * x0 + w2 * nxt

    def store_lhs(j, slab):
        for i in range(ni):
            jj = j + pad - 2 * i
            lhs_ref[nb * i:nb * (i + 1), LANES * jj:LANES * (jj + 1)] = slab

    def build_group(kf_ref, c, g, do_hi, do_lo):
        a = ng - g
        off = a * LANES
        if not isinstance(off, int):
            off = pl.multiple_of(off, LANES)
        tile_a = kfb_ref[:, pl.ds(off, LANES)]
        tile_b = kfb_ref[:, pl.ds(off + LANES, LANES)]
        row0 = g * LANES
        if not isinstance(row0, int):
            row0 = pl.multiple_of(row0, LANES)
        for e2 in range(LANES // 16):
            parts = []
            for e in (2 * e2, 2 * e2 + 1):
                ra = pltpu.roll(tile_a, 8 * e, axis=1, stride=1, stride_axis=0)
                rb = pltpu.roll(tile_b, 8 * e, axis=1, stride=1, stride_axis=0)
                parts.append(jnp.where(diff < 8 * e, ra, rb))
            blk = jnp.concatenate(parts, axis=0).astype(BF16)
            if do_hi:
                rhs_ref[pl.ds(pl.multiple_of(row0 + 16 * e2, 16), 16), LANES:2 * LANES] = blk
            if do_lo:
                rhs_ref[pl.ds(pl.multiple_of(row0 - LANES + 16 * e2, 16), 16), 0:LANES] = blk

    def build_rhs(kf_ref, c):
        base = pl.multiple_of(c // SUBLANES * SUBLANES, SUBLANES)
        pick = lax.broadcasted_iota(jnp.int32, (SUBLANES, LANES), 0) == c % SUBLANES
        for a in range(kfb_ref.shape[1] // LANES):
            tile = kf_ref[pl.ds(base, SUBLANES), LANES * a:LANES * (a + 1)]
            row = jnp.sum(jnp.where(pick, tile, 0.0), axis=0, keepdims=True)
            kfb_ref[:, LANES * a:LANES * (a + 1)] = jnp.broadcast_to(row, (SUBLANES, LANES))
        build_group(kf_ref, c, 0, True, False)

        def body(g, carry):
            build_group(kf_ref, c, g, True, True)
            return carry

        lax.fori_loop(1, ng, body, 0)
        build_group(kf_ref, c, ng, False, True)

    def channel(c, carry):
        par = lambda k: par_ref[c, pl.ds(k, 1), :]
        for j in range(nj):
            vc = short_conv(v_ref, c, j, par(8), par(9), par(10), par(11))
            vc_ref[j] = vc
            store_lhs(j, vc.astype(BF16))
        build_rhs(kf1_ref, c)
        y1 = _dot(lhs_ref[...], rhs_ref[...])
        for j in range(nj):
            i, half = divmod(j, 2)
            y = y1[nb * i:nb * (i + 1), LANES * half:LANES * (half + 1)]
            x1c = short_conv(x1_ref, c, j, par(0), par(1), par(2), par(3))
            z1 = x1c * (y + par(12) * vc_ref[j])
            z1_ref[j] = z1
            store_lhs(j, z1.astype(BF16))
        build_rhs(kf2_ref, c)
        y2 = _dot(lhs_ref[...], rhs_ref[...])
        for j in range(nj):
            i, half = divmod(j, 2)
            y = y2[nb * i:nb * (i + 1), LANES * half:LANES * (half + 1)]
            x2c = short_conv(x2_ref, c, j, par(4), par(5), par(6), par(7))
            o_ref[c, j] = (x2c * (y + par(13) * z1_ref[j])).astype(BF16)
        return carry

    lax.fori_loop(0, ct, channel, 0)


def _long_conv(ut, kf, par, width, ct):
    _, nj, nb, _ = ut.shape
    seq = nj * LANES
    kdim = 2 * seq - MXU_N
    nblk = width // ct
    uspec = lambda o: pl.BlockSpec((ct, nj, nb, LANES), lambda i: (i + o * nblk, 0, 0, 0))
    kspec = lambda n: pl.BlockSpec((None, ct, 2 * seq), lambda i: (n, i, 0))
    return pl.pallas_call(
        functools.partial(_conv_kernel, ct=ct),
        grid=(nblk,),
        in_specs=[uspec(0), uspec(1), uspec(2), kspec(0), kspec(1),
                  pl.BlockSpec((ct, 16, LANES), lambda i: (i, 0, 0))],
        out_specs=pl.BlockSpec((ct, nj, nb, LANES), lambda i: (i, 0, 0, 0)),
        out_shape=jax.ShapeDtypeStruct((width, nj, nb, LANES), BF16),
        scratch_shapes=[pltpu.VMEM((nj // 2 * nb, kdim), BF16),
                        pltpu.VMEM((kdim, MXU_N), BF16),
                        pltpu.VMEM((nj, nb, LANES), F32),
                        pltpu.VMEM((nj, nb, LANES), F32),
                        pltpu.VMEM((SUBLANES, 2 * seq), F32)],
        compiler_params=_params(("arbitrary",)),
        name="hyena_long_conv",
    )(ut, ut, ut, kf, kf, par)


def _out_kernel(z_ref, sg_ref, h_ref, w_ref, gf_ref, o_ref):
    zg = (z_ref[...].astype(F32) * sg_ref[...].astype(F32)).astype(BF16)
    h2 = h_ref[...] + _dot(zg, w_ref[...])
    o_ref[...] = _rms(h2, gf_ref[...])


def _hyena_out(z2d, sg2d, h2d, w, gf, tm):
    n, d = h2d.shape
    row = lambda w_: pl.BlockSpec((tm, w_), lambda i: (i, 0))
    full = lambda a: pl.BlockSpec(a.shape, lambda i: (0,) * a.ndim)
    return pl.pallas_call(
        _out_kernel,
        grid=(n // tm,),
        in_specs=[row(z2d.shape[1]), row(sg2d.shape[1]), row(d), full(w), full(gf)],
        out_specs=row(d),
        out_shape=jax.ShapeDtypeStruct((n, d), F32),
        compiler_params=_params(("parallel",)),
        name="hyena_out",
    )(z2d, sg2d, h2d, w, gf)


def _rope_tables(seq):
    inv = 1.0 / (ROPE_THETA ** (jnp.arange(0, ROPE, 2, dtype=F32) / ROPE))
    ang = jnp.arange(seq, dtype=F32)[:, None] * inv[None, :]
    cos, sin = jnp.cos(ang), jnp.sin(ang)
    zero = jnp.zeros((seq, LANES - ROPE), F32)
    return (jnp.concatenate([cos, cos, zero], axis=1), jnp.concatenate([-sin, sin, zero], axis=1))


def _filter_tables(seq, width):
    t = jnp.linspace(0.0, 1.0, seq, dtype=F32)[:, None]
    w = 2.0 * math.pi * jnp.arange(seq, dtype=F32) / seq
    bands = jnp.linspace(1e-4, POS_BANDS - 1, POS_BANDS, dtype=F32)
    fw = w[:, None] * bands[None, :]
    z = jnp.concatenate([t, jnp.cos(fw), -jnp.sin(fw)], axis=-1)
    deltas = jnp.abs(jnp.linspace(MIN_DECAY, MAX_DECAY, width, dtype=F32))
    decay = jnp.exp(-t * deltas[None, :])
    idx = np.abs(np.arange(2 * seq) - seq)
    idx[0] = 0
    zcat = jnp.pad(z[idx].T, ((0, FILT_HIDDEN - POS_EMB), (0, 0)))
    deccat = decay[idx].T.at[:, 0].set(0.0)
    return zcat, deccat


def kernel(x, l0_norm, l0_w_in, l0_q_norm, l0_w_uq, l0_kv_norm, l0_w_ukv, l0_w_out, l1_norm, l1_w_in,
           l1_conv_w, l1_conv_b, l1_filt_w1, l1_filt_b1, l1_filt_w2, l1_filt_b2, l1_filt_w3, l1_filt_b3,
           l1_filt_w4, l1_filt_freq, l1_filt_bias, l1_w_out, final_norm):
    b, seq, d = x.shape
    width = l1_w_out.shape[0]
    n = b * seq
    tm = min(512, seq)
    tq = min(256, seq)
    row = lambda a: a.reshape(1, -1).astype(F32)
    col = lambda a: a.reshape(-1, 1).astype(F32)

    s1, s2, s3 = Q_RANK, Q_RANK + KV_RANK, Q_RANK + KV_RANK + ROPE
    half = ROPE // 2
    zpad = jnp.zeros((d, LANES - ROPE), F32)
    kpe_w = l0_w_in[:, s2:s3]
    win = jnp.concatenate([l0_w_in[:, :s2], kpe_w, zpad, kpe_w[:, half:], kpe_w[:, :half], zpad,
                           l0_w_in[:, s3:]], axis=1).astype(BF16)
    wuq3 = l0_w_uq.reshape(Q_RANK, HEADS, QK_DIM)
    wuq = jnp.pad(wuq3, ((0, 0), (0, 0), (0, 2 * LANES - QK_DIM))).reshape(Q_RANK, -1).astype(BF16)
    wuqs = jnp.concatenate([wuq3[:, :, NOPE + half:], wuq3[:, :, NOPE:NOPE + half],
                            jnp.zeros((Q_RANK, HEADS, LANES - ROPE), F32)], axis=2)
    wuqs = wuqs.reshape(Q_RANK, -1).astype(BF16)
    wukv3 = l0_w_ukv.reshape(KV_RANK, HEADS, NOPE + VDIM)
    wukv = jnp.concatenate([wukv3[:, :, :NOPE].reshape(KV_RANK, -1),
                            wukv3[:, :, NOPE:].reshape(KV_RANK, -1)], axis=1).astype(BF16)
    cos_t, sin_t = _rope_tables(seq)

    q, k, v, g = _mla_in(x.reshape(n, d), row(l0_norm), win, row(l0_q_norm), wuq, wuqs, row(l0_kv_norm),
                         wukv, cos_t, sin_t, seq, tm)
    h1, h1n = _attention(q.reshape(b, seq, -1), k.reshape(b, seq, -1), v.reshape(b, seq, -1),
                         g.reshape(b, seq, -1), x, l0_w_out.astype(BF16), row(l1_norm), tq)

    u, sg = _hyena_in(h1n.reshape(n, d), l1_w_in.astype(BF16), width, tm)
    nj = seq // LANES
    ut = u.reshape(b, nj, LANES, 3 * width).transpose(3, 1, 0, 2)

    zcat, deccat = _filter_tables(seq, width)
    w4t = l1_filt_w4.T.reshape(2, 2, width, FILT_HIDDEN).astype(F32)
    w1t = jnp.pad(l1_filt_w1.T.astype(F32), ((0, 0), (0, FILT_HIDDEN - POS_EMB)))
    kf = _filters(zcat, w1t, col(l1_filt_b1), l1_filt_w2.T.astype(F32), col(l1_filt_b2),
                  l1_filt_w3.T.astype(F32), col(l1_filt_b3), col(l1_filt_freq), w4t, deccat,
                  min(256, width))

    cw = l1_conv_w.reshape(3, 3, width)
    cb = l1_conv_b.reshape(3, width)
    par = jnp.concatenate([cw[:, 0], cb[0:1], cw[:, 1], cb[1:2], cw[:, 2], cb[2:3],
                           l1_filt_bias, jnp.zeros((2, width), F32)], axis=0)
    par = jnp.broadcast_to(par.T[:, :, None], (width, 16, LANES)).astype(F32)

    z2t = _long_conv(ut, kf, par, width, 8)
    z2 = z2t.transpose(2, 1, 3, 0).reshape(n, width)
    out = _hyena_out(z2, sg, h1.reshape(n, d), l1_w_out.astype(BF16), row(final_norm), tm)
    return out.reshape(b, seq, d)
```

```python
import functools
import math

import jax
import jax.numpy as jnp
import numpy as np
from jax import lax
from jax.experimental import pallas as pl
from jax.experimental.pallas import tpu as pltpu

RMS_EPS = 1e-6
HEADS = 8
NOPE = 128
ROPE = 64
VDIM = 128
Q_RANK = 384
KV_RANK = 256
ROPE_THETA = 10000.0
QK_DIM = NOPE + ROPE

POS_EMB = 33
POS_BANDS = (POS_EMB - 1) // 2
FILT_HIDDEN = 64
MIN_DECAY = math.log(1e-2) / 1.5
MAX_DECAY = math.log(1e-2) / 0.3

LANES = 128
SUBLANES = 8
MXU_N = 256
VMEM_LIMIT = 56 * 1024 * 1024

F32 = jnp.float32
BF16 = jnp.bfloat16


def _rms(x, g):
    return x * lax.rsqrt(jnp.mean(x * x, axis=-1, keepdims=True) + RMS_EPS) * g


def _dot(a, b):
    return jnp.dot(a, b, preferred_element_type=F32)


def _params(sem):
    return pltpu.CompilerParams(dimension_semantics=sem, vmem_limit_bytes=VMEM_LIMIT)


def _mla_in_kernel(x_ref, g0_ref, win_ref, qn_ref, wuq_ref, wuqs_ref, kvn_ref, wukv_ref,
                   cos_ref, sin_ref, q_ref, k_ref, v_ref, g_ref):
    scale = QK_DIM ** -0.5
    xn = _rms(x_ref[...], g0_ref[...]).astype(BF16)
    proj = _dot(xn, win_ref[...])
    c_q = proj[:, :Q_RANK]
    c_kv = proj[:, Q_RANK:Q_RANK + KV_RANK]
    o = Q_RANK + KV_RANK
    kpe = proj[:, o:o + LANES]
    kpe_sw = proj[:, o + LANES:o + 2 * LANES]
    gate = proj[:, o + 2 * LANES:]
    cos = cos_ref[...]
    sin = sin_ref[...]

    cqn = _rms(c_q, qn_ref[...]).astype(BF16)
    qf = _dot(cqn, wuq_ref[...])
    qs = _dot(cqn, wuqs_ref[...])
    for h in range(HEADS):
        nope = qf[:, 2 * LANES * h:2 * LANES * h + LANES]
        pe = qf[:, 2 * LANES * h + LANES:2 * LANES * (h + 1)]
        pe = pe * cos + qs[:, LANES * h:LANES * (h + 1)] * sin
        q_ref[:, 2 * LANES * h:2 * LANES * h + LANES] = (nope * scale).astype(BF16)
        q_ref[:, 2 * LANES * h + LANES:2 * LANES * (h + 1)] = (pe * scale).astype(BF16)

    ckvn = _rms(c_kv, kvn_ref[...]).astype(BF16)
    kv = _dot(ckvn, wukv_ref[...])
    kpe_r = (kpe * cos + kpe_sw * sin).astype(BF16)
    for h in range(HEADS):
        k_ref[:, 2 * LANES * h:2 * LANES * h + LANES] = kv[:, LANES * h:LANES * (h + 1)].astype(BF16)
        k_ref[:, 2 * LANES * h + LANES:2 * LANES * (h + 1)] = kpe_r
    ones_col = (lax.broadcasted_iota(jnp.int32, (kv.shape[0], LANES), 1) == 0).astype(BF16)
    for h in range(HEADS):
        v_ref[:, 2 * LANES * h:2 * LANES * h + LANES] = kv[:, HEADS * NOPE + VDIM * h:
                                                           HEADS * NOPE + VDIM * (h + 1)].astype(BF16)
        v_ref[:, 2 * LANES * h + LANES:2 * LANES * (h + 1)] = ones_col
    g_ref[...] = (gate * jax.nn.sigmoid(gate)).astype(BF16)


def _mla_in(x2d, g0, win, qn, wuq, wuqs, kvn, wukv, cos_t, sin_t, seq, tm):
    n, d = x2d.shape
    nt = seq // tm
    full = lambda a: pl.BlockSpec(a.shape, lambda i: (0,) * a.ndim)
    return pl.pallas_call(
        _mla_in_kernel,
        grid=(n // tm,),
        in_specs=[pl.BlockSpec((tm, d), lambda i: (i, 0)), full(g0), full(win), full(qn), full(wuq),
                  full(wuqs), full(kvn), full(wukv),
                  pl.BlockSpec((tm, LANES), lambda i: (i % nt, 0)),
                  pl.BlockSpec((tm, LANES), lambda i: (i % nt, 0))],
        out_specs=[pl.BlockSpec((tm, HEADS * 2 * LANES), lambda i: (i, 0)),
                   pl.BlockSpec((tm, HEADS * 2 * LANES), lambda i: (i, 0)),
                   pl.BlockSpec((tm, HEADS * 2 * LANES), lambda i: (i, 0)),
                   pl.BlockSpec((tm, HEADS * VDIM), lambda i: (i, 0))],
        out_shape=[jax.ShapeDtypeStruct((n, HEADS * 2 * LANES), BF16),
                   jax.ShapeDtypeStruct((n, HEADS * 2 * LANES), BF16),
                   jax.ShapeDtypeStruct((n, HEADS * 2 * LANES), BF16),
                   jax.ShapeDtypeStruct((n, HEADS * VDIM), BF16)],
        compiler_params=_params(("parallel",)),
        name="mla_in",
    )(x2d, g0, win, qn, wuq, wuqs, kvn, wukv, cos_t, sin_t)


def _attn_kernel(q_ref, k_ref, v_ref, g_ref, x_ref, wout_ref, g1_ref, h_ref, hn_ref, o_ref):
    for h in range(HEADS):
        qh = q_ref[0, :, 2 * LANES * h:2 * LANES * (h + 1)]
        kh = k_ref[0, :, 2 * LANES * h:2 * LANES * (h + 1)]
        s = lax.dot_general(qh, kh, (((1,), (1,)), ((), ())), preferred_element_type=F32)
        m = jnp.max(s, axis=-1, keepdims=True)
        p = jnp.exp((s - m).astype(BF16))
        ov = _dot(p, v_ref[0, :, 2 * LANES * h:2 * LANES * (h + 1)])
        o_ref[:, VDIM * h:VDIM * (h + 1)] = ov[:, :VDIM] / ov[:, VDIM:VDIM + 1]
    og = (o_ref[...] * g_ref[0].astype(F32)).astype(BF16)
    h1 = x_ref[0] + _dot(og, wout_ref[...])
    h_ref[0] = h1
    hn_ref[0] = _rms(h1, g1_ref[...]).astype(BF16)


def _attention(q, k, v, g, x, wout, g1, tq):
    b, seq, d = x.shape
    qspec = lambda w: pl.BlockSpec((1, tq, w), lambda i, j: (i, j, 0))
    kspec = lambda w: pl.BlockSpec((1, seq, w), lambda i, j: (i, 0, 0))
    full = lambda a: pl.BlockSpec(a.shape, lambda i, j: (0,) * a.ndim)
    return pl.pallas_call(
        _attn_kernel,
        grid=(b, seq // tq),
        in_specs=[qspec(q.shape[-1]), kspec(k.shape[-1]), kspec(v.shape[-1]), qspec(g.shape[-1]),
                  qspec(d), full(wout), full(g1)],
        out_specs=[qspec(d), qspec(d)],
        out_shape=[jax.ShapeDtypeStruct((b, seq, d), F32), jax.ShapeDtypeStruct((b, seq, d), BF16)],
        scratch_shapes=[pltpu.VMEM((tq, HEADS * VDIM), F32)],
        compiler_params=_params(("parallel", "arbitrary")),
        name="mla_attn",
    )(q, k, v, g, x, wout, g1)


def _hyena_in_kernel(hn_ref, w_ref, cw_ref, o_ref, *, conv_blocks):
    seq = hn_ref.shape[1]
    proj = _dot(hn_ref[0], w_ref[...])

    @pl.when(pl.program_id(1) < conv_blocks)
    def _():
        w0, w1, w2, cb = cw_ref[0:1, :], cw_ref[1:2, :], cw_ref[2:3, :], cw_ref[3:4, :]
        prev = pltpu.roll(proj, 1, axis=0)
        nxt = pltpu.roll(proj, seq - 1, axis=0)
        o_ref[0] = (cb + w0 * prev + w1 * proj + w2 * nxt).astype(BF16)
        rows = lax.broadcasted_iota(jnp.int32, (16, proj.shape[1]), 0)
        top = cb + w0 * jnp.where(rows == 0, 0.0, prev[:16]) + w1 * proj[:16] + w2 * nxt[:16]
        o_ref[0, :16, :] = top.astype(BF16)
        bot = (cb + w0 * prev[seq - 16:] + w1 * proj[seq - 16:]
               + w2 * jnp.where(rows == 15, 0.0, nxt[seq - 16:]))
        o_ref[0, seq - 16:, :] = bot.astype(BF16)

    @pl.when(pl.program_id(1) >= conv_blocks)
    def _():
        o_ref[0] = (proj * jax.nn.sigmoid(proj)).astype(BF16)


def _hyena_in(hn, w, cw, tn, conv_blocks):
    b, seq, d = hn.shape
    nout = w.shape[1]
    return pl.pallas_call(
        functools.partial(_hyena_in_kernel, conv_blocks=conv_blocks),
        grid=(b, nout // tn),
        in_specs=[pl.BlockSpec((1, seq, d), lambda i, j: (i, 0, 0)),
                  pl.BlockSpec((d, tn), lambda i, j: (0, j)),
                  pl.BlockSpec((SUBLANES, tn), lambda i, j: (0, j))],
        out_specs=pl.BlockSpec((1, seq, tn), lambda i, j: (i, 0, j)),
        out_shape=jax.ShapeDtypeStruct((b, seq, nout), BF16),
        compiler_params=_params(("parallel", "arbitrary")),
        name="hyena_in",
    )(hn, w, cw)


def _bf16_bits(x):
    bits = lax.bitcast_convert_type(x, jnp.uint32)
    return (bits + jnp.uint32(0x7FFF) + ((bits >> 16) & jnp.uint32(1))) >> 16


def _filter_kernel(z_ref, w1_ref, b1_ref, w2_ref, b2_ref, w3_ref, b3_ref, fr_ref, w4_ref, dec_ref,
                   kf_ref, a_ref):
    seq = a_ref.shape[1] // 2
    hi = lax.Precision.HIGHEST

    @pl.when((pl.program_id(0) == 0) & (pl.program_id(1) == 0))
    def _():
        fr = fr_ref[...]
        a = jnp.sin(fr * (jnp.dot(w1_ref[...], z_ref[...], precision=hi, preferred_element_type=F32)
                          + b1_ref[...]))
        a = jnp.sin(fr * (jnp.dot(w2_ref[...], a, precision=hi, preferred_element_type=F32) + b2_ref[...]))
        a = jnp.sin(fr * (jnp.dot(w3_ref[...], a, precision=hi, preferred_element_type=F32) + b3_ref[...]))
        a_ref[...] = a

    bwd = jnp.dot(w4_ref[0, 1], a_ref[:, :seq], precision=hi, preferred_element_type=F32)
    fwd = jnp.dot(w4_ref[0, 0], a_ref[:, seq:], precision=hi, preferred_element_type=F32)
    kf = jnp.concatenate([bwd, fwd], axis=1) * dec_ref[...]
    kf_ref[0] = _bf16_bits(kf) | (_bf16_bits(pltpu.roll(kf, 1, axis=1)) << 16)


def _filters(zcat, w1t, b1, w2t, b2, w3t, b3, fr, w4t, deccat, tc):
    _, _, width, hid = w4t.shape
    two_l = zcat.shape[1]
    full = lambda a: pl.BlockSpec(a.shape, lambda n, i: (0,) * a.ndim)
    return pl.pallas_call(
        _filter_kernel,
        grid=(2, width // tc),
        in_specs=[full(zcat), full(w1t), full(b1), full(w2t), full(b2), full(w3t), full(b3), full(fr),
                  pl.BlockSpec((1, 2, tc, hid), lambda n, i: (n, 0, i, 0)),
                  pl.BlockSpec((tc, two_l), lambda n, i: (i, 0))],
        out_specs=pl.BlockSpec((1, tc, two_l), lambda n, i: (n, i, 0)),
        out_shape=jax.ShapeDtypeStruct((2, width, two_l), jnp.uint32),
        scratch_shapes=[pltpu.VMEM((hid, two_l), F32)],
        compiler_params=_params(("arbitrary", "arbitrary")),
        name="hyena_filters",
    )(zcat, w1t, b1, w2t, b2, w3t, b3, fr, w4t, deccat)


def _conv_kernel(x1_ref, x2_ref, v_ref, kf1_ref, kf2_ref, par_ref, o_ref,
                 lhs_ref, rhs1_ref, rhs2_ref, z1_ref, *, ct):
    nj, nb = v_ref.shape[1], v_ref.shape[2]
    ni = nj // 2
    ntile = kf1_ref.shape[1]
    ng = ntile - 2
    pad = 2 * ni - 2
    rows_per_word_tile = 2 * SUBLANES

    @pl.when(pl.program_id(0) == 0)
    def _():
        lhs_ref[...] = jnp.zeros(lhs_ref.shape, lhs_ref.dtype)

    diff = (lax.broadcasted_iota(jnp.int32, (SUBLANES, LANES), 1)
            - 2 * lax.broadcasted_iota(jnp.int32, (SUBLANES, LANES), 0))
    nroll = LANES // rows_per_word_tile

    def store_lhs(s, j, slab):
        for i in range(ni):
            jj = j + pad - 2 * i
            lhs_ref[s, nb * i:nb * (i + 1), LANES * jj:LANES * (jj + 1)] = slab

    def build_rhs(kf_ref, c, rhs_ref):
        prev = None
        for x in range(ntile - 1, -1, -1):
            tile = jnp.broadcast_to(kf_ref[c, x:x + 1, :], (SUBLANES, LANES))
            cur = [pltpu.roll(tile, rows_per_word_tile * e, axis=1, stride=2, stride_axis=0)
                   for e in range(nroll)]
            if prev is not None:
                g = ng - x
                for e in range(nroll):
                    words = jnp.where(diff < rows_per_word_tile * e, cur[e], prev[e])
                    blk = pltpu.bitcast(words, BF16)
                    r = LANES * g + rows_per_word_tile * e
                    if g < ng:
                        rhs_ref[r:r + rows_per_word_tile, LANES:2 * LANES] = blk
                    if g >= 1:
                        rhs_ref[r - LANES:r - LANES + rows_per_word_tile, 0:LANES] = blk
            prev = cur

    def piece(y, j):
        i, half = divmod(j, 2)
        return y[nb * i:nb * (i + 1), LANES * half:LANES * (half + 1)]

    slots = (0, 1)
    for s in slots:
        build_rhs(kf1_ref, s, rhs1_ref.at[s])

    def channel_pair(p, carry):
        cs = [2 * p + s for s in slots]
        nxt = [jnp.minimum(c + 2, ct - 2 + s) for s, c in zip(slots, cs)]
        bias1 = [par_ref[c, 0:1, :] for c in cs]
        bias2 = [par_ref[c, 1:2, :] for c in cs]
        for s, c in zip(slots, cs):
            for j in range(nj):
                store_lhs(s, j, v_ref[c, j])
        y1 = [_dot(lhs_ref[s], rhs1_ref[s]) for s in slots]
        for s, c in zip(slots, cs):
            build_rhs(kf2_ref, c, rhs2_ref.at[s])
        for s, c in zip(slots, cs):
            for j in range(nj):
                z1 = x1_ref[c, j].astype(F32) * (piece(y1[s], j) + bias1[s] * v_ref[c, j].astype(F32))
                z1_ref[s, j] = z1
                store_lhs(s, j, z1.astype(BF16))
        y2 = [_dot(lhs_ref[s], rhs2_ref[s]) for s in slots]
        for s in slots:
            build_rhs(kf1_ref, nxt[s], rhs1_ref.at[s])
        for s, c in zip(slots, cs):
            for j in range(nj):
                z2 = x2_ref[c, j].astype(F32) * (piece(y2[s], j) + bias2[s] * z1_ref[s, j])
                o_ref[c, j] = z2.astype(BF16)
        return carry

    lax.fori_loop(0, ct // 2, channel_pair, 0)


def _long_conv(ut, kf, par, width, ct):
    _, nj, nb, _ = ut.shape
    seq = nj * LANES
    kdim = 2 * seq - MXU_N
    nblk = width // ct
    ntile = kf.shape[2]
    uspec = lambda o: pl.BlockSpec((ct, nj, nb, LANES), lambda i: (i + o * nblk, 0, 0, 0))
    kspec = lambda n: pl.BlockSpec((None, ct, ntile, LANES), lambda i: (n, i, 0, 0))
    return pl.pallas_call(
        functools.partial(_conv_kernel, ct=ct),
        grid=(nblk,),
        in_specs=[uspec(0), uspec(1), uspec(2), kspec(0), kspec(1),
                  pl.BlockSpec((ct, SUBLANES, LANES), lambda i: (i, 0, 0))],
        out_specs=pl.BlockSpec((ct, nj, nb, LANES), lambda i: (i, 0, 0, 0)),
        out_shape=jax.ShapeDtypeStruct((width, nj, nb, LANES), BF16),
        scratch_shapes=[pltpu.VMEM((2, nj // 2 * nb, kdim), BF16),
                        pltpu.VMEM((2, kdim, MXU_N), BF16),
                        pltpu.VMEM((2, kdim, MXU_N), BF16),
                        pltpu.VMEM((2, nj, nb, LANES), F32)],
        compiler_params=_params(("arbitrary",)),
        name="hyena_long_conv",
    )(ut, ut, ut, kf, kf, par)


def _out_kernel(z_ref, sg_ref, h_ref, w_ref, gf_ref, o_ref):
    zg = (z_ref[...].astype(F32) * sg_ref[...].astype(F32)).astype(BF16)
    h2 = h_ref[...] + _dot(zg, w_ref[...])
    o_ref[...] = _rms(h2, gf_ref[...])


def _hyena_out(z2d, u2d, h2d, w, gf, tm):
    n, d = h2d.shape
    width = z2d.shape[1]
    gate_block = u2d.shape[1] // width - 1
    row = lambda w_: pl.BlockSpec((tm, w_), lambda i: (i, 0))
    full = lambda a: pl.BlockSpec(a.shape, lambda i: (0,) * a.ndim)
    return pl.pallas_call(
        _out_kernel,
        grid=(n // tm,),
        in_specs=[row(width), pl.BlockSpec((tm, width), lambda i: (i, gate_block)), row(d), full(w), full(gf)],
        out_specs=row(d),
        out_shape=jax.ShapeDtypeStruct((n, d), F32),
        compiler_params=_params(("parallel",)),
        name="hyena_out",
    )(z2d, u2d, h2d, w, gf)


def _rope_tables(seq):
    inv = 1.0 / (ROPE_THETA ** (jnp.arange(0, ROPE, 2, dtype=F32) / ROPE))
    ang = jnp.arange(seq, dtype=F32)[:, None] * inv[None, :]
    cos, sin = jnp.cos(ang), jnp.sin(ang)
    zero = jnp.zeros((seq, LANES - ROPE), F32)
    return (jnp.concatenate([cos, cos, zero], axis=1), jnp.concatenate([-sin, sin, zero], axis=1))


def _filter_tables(seq, width):
    t = jnp.linspace(0.0, 1.0, seq, dtype=F32)[:, None]
    w = 2.0 * math.pi * jnp.arange(seq, dtype=F32) / seq
    bands = jnp.linspace(1e-4, POS_BANDS - 1, POS_BANDS, dtype=F32)
    fw = w[:, None] * bands[None, :]
    z = jnp.concatenate([t, jnp.cos(fw), -jnp.sin(fw)], axis=-1)
    deltas = jnp.abs(jnp.linspace(MIN_DECAY, MAX_DECAY, width, dtype=F32))
    decay = jnp.exp(-t * deltas[None, :])
    idx = np.abs(np.arange(2 * seq) - seq)
    idx[0] = 0
    zcat = jnp.pad(z[idx].T, ((0, FILT_HIDDEN - POS_EMB), (0, 0)))
    deccat = decay[idx].T.at[:, 0].set(0.0)
    return zcat, deccat


def kernel(x, l0_norm, l0_w_in, l0_q_norm, l0_w_uq, l0_kv_norm, l0_w_ukv, l0_w_out, l1_norm, l1_w_in,
           l1_conv_w, l1_conv_b, l1_filt_w1, l1_filt_b1, l1_filt_w2, l1_filt_b2, l1_filt_w3, l1_filt_b3,
           l1_filt_w4, l1_filt_freq, l1_filt_bias, l1_w_out, final_norm):
    b, seq, d = x.shape
    width = l1_w_out.shape[0]
    n = b * seq
    tm = min(512, seq)
    tq = min(256, seq)
    row = lambda a: a.reshape(1, -1).astype(F32)
    col = lambda a: a.reshape(-1, 1).astype(F32)

    s1, s2, s3 = Q_RANK, Q_RANK + KV_RANK, Q_RANK + KV_RANK + ROPE
    half = ROPE // 2
    zpad = jnp.zeros((d, LANES - ROPE), F32)
    kpe_w = l0_w_in[:, s2:s3]
    win = jnp.concatenate([l0_w_in[:, :s2], kpe_w, zpad, kpe_w[:, half:], kpe_w[:, :half], zpad,
                           l0_w_in[:, s3:]], axis=1).astype(BF16)
    wuq3 = l0_w_uq.reshape(Q_RANK, HEADS, QK_DIM)
    wuq = jnp.pad(wuq3, ((0, 0), (0, 0), (0, 2 * LANES - QK_DIM))).reshape(Q_RANK, -1).astype(BF16)
    wuqs = jnp.concatenate([wuq3[:, :, NOPE + half:], wuq3[:, :, NOPE:NOPE + half],
                            jnp.zeros((Q_RANK, HEADS, LANES - ROPE), F32)], axis=2)
    wuqs = wuqs.reshape(Q_RANK, -1).astype(BF16)
    wukv3 = l0_w_ukv.reshape(KV_RANK, HEADS, NOPE + VDIM)
    wukv = jnp.concatenate([wukv3[:, :, :NOPE].reshape(KV_RANK, -1),
                            wukv3[:, :, NOPE:].reshape(KV_RANK, -1)], axis=1).astype(BF16)
    cos_t, sin_t = _rope_tables(seq)

    q, k, v, g = _mla_in(x.reshape(n, d), row(l0_norm), win, row(l0_q_norm), wuq, wuqs, row(l0_kv_norm),
                         wukv, cos_t, sin_t, seq, tm)
    h1, h1n = _attention(q.reshape(b, seq, -1), k.reshape(b, seq, -1), v.reshape(b, seq, -1),
                         g.reshape(b, seq, -1), x, l0_w_out.astype(BF16), row(l1_norm), tq)

    tn = 512
    nu = 3 * width
    cw = jnp.concatenate([l1_conv_w, l1_conv_b[None, :], jnp.zeros((SUBLANES - 4, nu), F32)], axis=0)
    cw = jnp.pad(cw.astype(F32), ((0, 0), (0, width)))
    u = _hyena_in(h1n, l1_w_in.astype(BF16), cw, tn, nu // tn)
    nj = seq // LANES
    ut = u[:, :, :nu].reshape(b, nj, LANES, nu).transpose(3, 1, 0, 2)

    zcat, deccat = _filter_tables(seq, width)
    w4t = l1_filt_w4.T.reshape(2, 2, width, FILT_HIDDEN).astype(F32)
    w1t = jnp.pad(l1_filt_w1.T.astype(F32), ((0, 0), (0, FILT_HIDDEN - POS_EMB)))
    kf = _filters(zcat, w1t, col(l1_filt_b1), l1_filt_w2.T.astype(F32), col(l1_filt_b2),
                  l1_filt_w3.T.astype(F32), col(l1_filt_b3), col(l1_filt_freq), w4t, deccat,
                  min(256, width))
    kf = kf.reshape(2, width, 2 * seq // LANES, LANES)

    par = jnp.concatenate([l1_filt_bias.astype(F32), jnp.zeros((SUBLANES - 2, width), F32)], axis=0)
    par = jnp.broadcast_to(par.T[:, :, None], (width, SUBLANES, LANES))

    z2t = _long_conv(ut, kf, par, width, 16)
    z2 = z2t.transpose(2, 1, 3, 0).reshape(n, width)
    out = _hyena_out(z2, u.reshape(n, 4 * width), h1.reshape(n, d), l1_w_out.astype(BF16),
                     row(final_norm), tm)
    return out.reshape(b, seq, d)
```

```python
import functools
import math

import jax
import jax.numpy as jnp
import numpy as np
from jax import lax
from jax.experimental import pallas as pl
from jax.experimental.pallas import tpu as pltpu

RMS_EPS = 1e-6
HEADS = 8
NOPE = 128
ROPE = 64
VDIM = 128
Q_RANK = 384
KV_RANK = 256
ROPE_THETA = 10000.0
QK_DIM = NOPE + ROPE

POS_EMB = 33
POS_BANDS = (POS_EMB - 1) // 2
FILT_HIDDEN = 64
MIN_DECAY = math.log(1e-2) / 1.5
MAX_DECAY = math.log(1e-2) / 0.3

LANES = 128
SUBLANES = 8
MXU_N = 256
VMEM_LIMIT = 56 * 1024 * 1024

F32 = jnp.float32
BF16 = jnp.bfloat16


def _rms(x, g):
    return x * lax.rsqrt(jnp.mean(x * x, axis=-1, keepdims=True) + RMS_EPS) * g


def _dot(a, b):
    return jnp.dot(a, b, preferred_element_type=F32)


def _params(sem):
    return pltpu.CompilerParams(dimension_semantics=sem, vmem_limit_bytes=VMEM_LIMIT)


def _mla_in_kernel(x_ref, g0_ref, win_ref, qn_ref, wuq_ref, wuqs_ref, kvn_ref, wukv_ref,
                   cos_ref, sin_ref, q_ref, k_ref, v_ref, g_ref):
    scale = QK_DIM ** -0.5
    xn = _rms(x_ref[...], g0_ref[...]).astype(BF16)
    proj = _dot(xn, win_ref[...])
    c_q = proj[:, :Q_RANK]
    c_kv = proj[:, Q_RANK:Q_RANK + KV_RANK]
    o = Q_RANK + KV_RANK
    kpe = proj[:, o:o + LANES]
    kpe_sw = proj[:, o + LANES:o + 2 * LANES]
    gate = proj[:, o + 2 * LANES:]
    cos = cos_ref[...]
    sin = sin_ref[...]

    cqn = _rms(c_q, qn_ref[...]).astype(BF16)
    qf = _dot(cqn, wuq_ref[...])
    qs = _dot(cqn, wuqs_ref[...])
    for h in range(HEADS):
        nope = qf[:, 2 * LANES * h:2 * LANES * h + LANES]
        pe = qf[:, 2 * LANES * h + LANES:2 * LANES * (h + 1)]
        pe = pe * cos + qs[:, LANES * h:LANES * (h + 1)] * sin
        q_ref[:, 2 * LANES * h:2 * LANES * h + LANES] = (nope * scale).astype(BF16)
        q_ref[:, 2 * LANES * h + LANES:2 * LANES * (h + 1)] = (pe * scale).astype(BF16)

    ckvn = _rms(c_kv, kvn_ref[...]).astype(BF16)
    kv = _dot(ckvn, wukv_ref[...])
    kpe_r = (kpe * cos + kpe_sw * sin).astype(BF16)
    for h in range(HEADS):
        k_ref[:, 2 * LANES * h:2 * LANES * h + LANES] = kv[:, LANES * h:LANES * (h + 1)].astype(BF16)
        k_ref[:, 2 * LANES * h + LANES:2 * LANES * (h + 1)] = kpe_r
    ones_col = (lax.broadcasted_iota(jnp.int32, (kv.shape[0], LANES), 1) == 0).astype(BF16)
    for h in range(HEADS):
        v_ref[:, 2 * LANES * h:2 * LANES * h + LANES] = kv[:, HEADS * NOPE + VDIM * h:
                                                           HEADS * NOPE + VDIM * (h + 1)].astype(BF16)
        v_ref[:, 2 * LANES * h + LANES:2 * LANES * (h + 1)] = ones_col
    g_ref[...] = (gate * jax.nn.sigmoid(gate)).astype(BF16)


def _mla_in(x2d, g0, win, qn, wuq, wuqs, kvn, wukv, cos_t, sin_t, seq, tm):
    n, d = x2d.shape
    nt = seq // tm
    full = lambda a: pl.BlockSpec(a.shape, lambda i: (0,) * a.ndim)
    return pl.pallas_call(
        _mla_in_kernel,
        grid=(n // tm,),
        in_specs=[pl.BlockSpec((tm, d), lambda i: (i, 0)), full(g0), full(win), full(qn), full(wuq),
                  full(wuqs), full(kvn), full(wukv),
                  pl.BlockSpec((tm, LANES), lambda i: (i % nt, 0)),
                  pl.BlockSpec((tm, LANES), lambda i: (i % nt, 0))],
        out_specs=[pl.BlockSpec((tm, HEADS * 2 * LANES), lambda i: (i, 0)),
                   pl.BlockSpec((tm, HEADS * 2 * LANES), lambda i: (i, 0)),
                   pl.BlockSpec((tm, HEADS * 2 * LANES), lambda i: (i, 0)),
                   pl.BlockSpec((tm, HEADS * VDIM), lambda i: (i, 0))],
        out_shape=[jax.ShapeDtypeStruct((n, HEADS * 2 * LANES), BF16),
                   jax.ShapeDtypeStruct((n, HEADS * 2 * LANES), BF16),
                   jax.ShapeDtypeStruct((n, HEADS * 2 * LANES), BF16),
                   jax.ShapeDtypeStruct((n, HEADS * VDIM), BF16)],
        compiler_params=_params(("parallel",)),
        name="mla_in",
    )(x2d, g0, win, qn, wuq, wuqs, kvn, wukv, cos_t, sin_t)


def _attn_kernel(q_ref, k_ref, v_ref, g_ref, x_ref, wout_ref, g1_ref, h_ref, hn_ref, o_ref):
    for h in range(HEADS):
        qh = q_ref[0, :, 2 * LANES * h:2 * LANES * (h + 1)]
        kh = k_ref[0, :, 2 * LANES * h:2 * LANES * (h + 1)]
        s = lax.dot_general(qh, kh, (((1,), (1,)), ((), ())), preferred_element_type=F32)
        m = jnp.max(s, axis=-1, keepdims=True)
        p = jnp.exp((s - m).astype(BF16))
        ov = _dot(p, v_ref[0, :, 2 * LANES * h:2 * LANES * (h + 1)])
        o_ref[:, VDIM * h:VDIM * (h + 1)] = ov[:, :VDIM] / ov[:, VDIM:VDIM + 1]
    og = (o_ref[...] * g_ref[0].astype(F32)).astype(BF16)
    h1 = x_ref[0] + _dot(og, wout_ref[...])
    h_ref[0] = h1
    hn_ref[0] = _rms(h1, g1_ref[...]).astype(BF16)


def _attention(q, k, v, g, x, wout, g1, tq):
    b, seq, d = x.shape
    qspec = lambda w: pl.BlockSpec((1, tq, w), lambda i, j: (i, j, 0))
    kspec = lambda w: pl.BlockSpec((1, seq, w), lambda i, j: (i, 0, 0))
    full = lambda a: pl.BlockSpec(a.shape, lambda i, j: (0,) * a.ndim)
    return pl.pallas_call(
        _attn_kernel,
        grid=(b, seq // tq),
        in_specs=[qspec(q.shape[-1]), kspec(k.shape[-1]), kspec(v.shape[-1]), qspec(g.shape[-1]),
                  qspec(d), full(wout), full(g1)],
        out_specs=[qspec(d), qspec(d)],
        out_shape=[jax.ShapeDtypeStruct((b, seq, d), F32), jax.ShapeDtypeStruct((b, seq, d), BF16)],
        scratch_shapes=[pltpu.VMEM((tq, HEADS * VDIM), F32)],
        compiler_params=_params(("parallel", "arbitrary")),
        name="mla_attn",
    )(q, k, v, g, x, wout, g1)


def _hyena_in_kernel(hn_ref, w_ref, cw_ref, o_ref, *, rc):
    seq, tn = hn_ref.shape[1], w_ref.shape[1]
    halo = 16
    w0, w1, w2, cb = cw_ref[0:1, :], cw_ref[1:2, :], cw_ref[2:3, :], cw_ref[3:4, :]
    rows = lax.broadcasted_iota(jnp.int32, (halo, tn), 0)
    nchunk = seq // rc
    for r in range(nchunk):
        lo, hi = max(rc * r - halo, 0), min(rc * (r + 1) + halo, seq)
        n = hi - lo
        proj = _dot(hn_ref[0, lo:hi, :], w_ref[...])
        prev = pltpu.roll(proj, 1, axis=0)
        nxt = pltpu.roll(proj, n - 1, axis=0)
        out = cb + w0 * prev + w1 * proj + w2 * nxt
        first = rc * r - lo
        o_ref[0, rc * r:rc * (r + 1), :] = out[first:first + rc].astype(BF16)
        if r == 0:
            top = cb + w0 * jnp.where(rows == 0, 0.0, prev[:halo]) + w1 * proj[:halo] + w2 * nxt[:halo]
            o_ref[0, :halo, :] = top.astype(BF16)
        if r == nchunk - 1:
            bot = (cb + w0 * prev[n - halo:] + w1 * proj[n - halo:]
                   + w2 * jnp.where(rows == halo - 1, 0.0, nxt[n - halo:]))
            o_ref[0, seq - halo:, :] = bot.astype(BF16)


def _hyena_in(hn, w, cw, tn, rc):
    b, seq, d = hn.shape
    nout = w.shape[1]
    return pl.pallas_call(
        functools.partial(_hyena_in_kernel, rc=rc),
        grid=(b, nout // tn),
        in_specs=[pl.BlockSpec((1, seq, d), lambda i, j: (i, 0, 0)),
                  pl.BlockSpec((d, tn), lambda i, j: (0, j)),
                  pl.BlockSpec((SUBLANES, tn), lambda i, j: (0, j))],
        out_specs=pl.BlockSpec((1, seq, tn), lambda i, j: (i, 0, j)),
        out_shape=jax.ShapeDtypeStruct((b, seq, nout), BF16),
        compiler_params=_params(("parallel", "arbitrary")),
        name="hyena_in",
    )(hn, w, cw)


def _bf16_bits(x):
    bits = lax.bitcast_convert_type(x, jnp.uint32)
    return (bits + jnp.uint32(0x7FFF) + ((bits >> 16) & jnp.uint32(1))) >> 16


def _filter_kernel(z_ref, w1_ref, b1_ref, w2_ref, b2_ref, w3_ref, b3_ref, fr_ref, w4_ref, dec_ref,
                   kf_ref, a_ref):
    seq = a_ref.shape[1] // 2
    hi = lax.Precision.HIGHEST

    @pl.when((pl.program_id(0) == 0) & (pl.program_id(1) == 0))
    def _():
        fr = fr_ref[...]
        a = jnp.sin(fr * (jnp.dot(w1_ref[...], z_ref[...], precision=hi, preferred_element_type=F32)
                          + b1_ref[...]))
        a = jnp.sin(fr * (jnp.dot(w2_ref[...], a, precision=hi, preferred_element_type=F32) + b2_ref[...]))
        a = jnp.sin(fr * (jnp.dot(w3_ref[...], a, precision=hi, preferred_element_type=F32) + b3_ref[...]))
        a_ref[...] = a

    bwd = jnp.dot(w4_ref[0, 1], a_ref[:, :seq], precision=hi, preferred_element_type=F32)
    fwd = jnp.dot(w4_ref[0, 0], a_ref[:, seq:], precision=hi, preferred_element_type=F32)
    kf = jnp.concatenate([bwd, fwd], axis=1) * dec_ref[...]
    kf_ref[0] = _bf16_bits(kf) | (_bf16_bits(pltpu.roll(kf, 1, axis=1)) << 16)


def _filters(zcat, w1t, b1, w2t, b2, w3t, b3, fr, w4t, deccat, tc):
    _, _, width, hid = w4t.shape
    two_l = zcat.shape[1]
    full = lambda a: pl.BlockSpec(a.shape, lambda n, i: (0,) * a.ndim)
    return pl.pallas_call(
        _filter_kernel,
        grid=(2, width // tc),
        in_specs=[full(zcat), full(w1t), full(b1), full(w2t), full(b2), full(w3t), full(b3), full(fr),
                  pl.BlockSpec((1, 2, tc, hid), lambda n, i: (n, 0, i, 0)),
                  pl.BlockSpec((tc, two_l), lambda n, i: (i, 0))],
        out_specs=pl.BlockSpec((1, tc, two_l), lambda n, i: (n, i, 0)),
        out_shape=jax.ShapeDtypeStruct((2, width, two_l), jnp.uint32),
        scratch_shapes=[pltpu.VMEM((hid, two_l), F32)],
        compiler_params=_params(("arbitrary", "arbitrary")),
        name="hyena_filters",
    )(zcat, w1t, b1, w2t, b2, w3t, b3, fr, w4t, deccat)


def _conv_kernel(x1_ref, x2_ref, v_ref, kf1_ref, kf2_ref, par_ref, o_ref,
                 lhs_ref, rhs1_ref, rhs2_ref, z1_ref, *, ct):
    nj, nb = v_ref.shape[1], v_ref.shape[2]
    ni = nj // 2
    ntile = kf1_ref.shape[1]
    ng = ntile - 2
    pad = 2 * ni - 2
    rows_per_word_tile = 2 * SUBLANES

    @pl.when(pl.program_id(0) == 0)
    def _():
        lhs_ref[...] = jnp.zeros(lhs_ref.shape, lhs_ref.dtype)

    diff = (lax.broadcasted_iota(jnp.int32, (SUBLANES, LANES), 1)
            - 2 * lax.broadcasted_iota(jnp.int32, (SUBLANES, LANES), 0))
    nroll = LANES // rows_per_word_tile

    def store_lhs(s, j, slab):
        for i in range(ni):
            jj = j + pad - 2 * i
            lhs_ref[s, nb * i:nb * (i + 1), LANES * jj:LANES * (jj + 1)] = slab

    def build_rhs(kf_ref, c, rhs_ref):
        prev = None
        for x in range(ntile - 1, -1, -1):
            tile = jnp.broadcast_to(kf_ref[c, x:x + 1, :], (SUBLANES, LANES))
            cur = [pltpu.roll(tile, rows_per_word_tile * e, axis=1, stride=2, stride_axis=0)
                   for e in range(nroll)]
            if prev is not None:
                g = ng - x
                for e in range(nroll):
                    words = jnp.where(diff < rows_per_word_tile * e, cur[e], prev[e])
                    blk = pltpu.bitcast(words, BF16)
                    r = LANES * g + rows_per_word_tile * e
                    if g < ng:
                        rhs_ref[r:r + rows_per_word_tile, LANES:2 * LANES] = blk
                    if g >= 1:
                        rhs_ref[r - LANES:r - LANES + rows_per_word_tile, 0:LANES] = blk
            prev = cur

    def piece(y, j):
        i, half = divmod(j, 2)
        return y[nb * i:nb * (i + 1), LANES * half:LANES * (half + 1)]

    slots = (0, 1)
    for s in slots:
        build_rhs(kf1_ref, s, rhs1_ref.at[s])

    def channel_pair(p, carry):
        cs = [2 * p + s for s in slots]
        nxt = [jnp.minimum(c + 2, ct - 2 + s) for s, c in zip(slots, cs)]
        bias1 = [par_ref[c, 0:1, :] for c in cs]
        bias2 = [par_ref[c, 1:2, :] for c in cs]
        for s, c in zip(slots, cs):
            for j in range(nj):
                store_lhs(s, j, v_ref[c, j])
        y1 = [_dot(lhs_ref[s], rhs1_ref[s]) for s in slots]
        for s, c in zip(slots, cs):
            build_rhs(kf2_ref, c, rhs2_ref.at[s])
        for s, c in zip(slots, cs):
            for j in range(nj):
                z1 = x1_ref[c, j].astype(F32) * (piece(y1[s], j) + bias1[s] * v_ref[c, j].astype(F32))
                z1_ref[s, j] = z1
                store_lhs(s, j, z1.astype(BF16))
        y2 = [_dot(lhs_ref[s], rhs2_ref[s]) for s in slots]
        for s in slots:
            build_rhs(kf1_ref, nxt[s], rhs1_ref.at[s])
        for s, c in zip(slots, cs):
            for j in range(nj):
                z2 = x2_ref[c, j].astype(F32) * (piece(y2[s], j) + bias2[s] * z1_ref[s, j])
                o_ref[c, j] = z2.astype(BF16)
        return carry

    lax.fori_loop(0, ct // 2, channel_pair, 0)


def _long_conv(ut, kf, par, width, ct):
    _, nj, nb, _ = ut.shape
    seq = nj * LANES
    kdim = 2 * seq - MXU_N
    nblk = width // ct
    ntile = kf.shape[2]
    uspec = lambda o: pl.BlockSpec((ct, nj, nb, LANES), lambda i: (i + o * nblk, 0, 0, 0))
    kspec = lambda n: pl.BlockSpec((None, ct, ntile, LANES), lambda i: (n, i, 0, 0))
    return pl.pallas_call(
        functools.partial(_conv_kernel, ct=ct),
        grid=(nblk,),
        in_specs=[uspec(0), uspec(1), uspec(2), kspec(0), kspec(1),
                  pl.BlockSpec((ct, SUBLANES, LANES), lambda i: (i, 0, 0))],
        out_specs=pl.BlockSpec((ct, nj, nb, LANES), lambda i: (i, 0, 0, 0)),
        out_shape=jax.ShapeDtypeStruct((width, nj, nb, LANES), BF16),
        scratch_shapes=[pltpu.VMEM((2, nj // 2 * nb, kdim), BF16),
                        pltpu.VMEM((2, kdim, MXU_N), BF16),
                        pltpu.VMEM((2, kdim, MXU_N), BF16),
                        pltpu.VMEM((2, nj, nb, LANES), F32)],
        compiler_params=_params(("arbitrary",)),
        name="hyena_long_conv",
    )(ut, ut, ut, kf, kf, par)


def _out_kernel(z_ref, hn_ref, h_ref, wg_ref, w_ref, gf_ref, o_ref):
    gate = _dot(hn_ref[...], wg_ref[...])
    zg = (z_ref[...].astype(F32) * (gate * jax.nn.sigmoid(gate))).astype(BF16)
    h2 = h_ref[...] + _dot(zg, w_ref[...])
    o_ref[...] = _rms(h2, gf_ref[...])


def _hyena_out(z2d, hn2d, h2d, wg, w, gf, tm):
    n, d = h2d.shape
    row = lambda w_: pl.BlockSpec((tm, w_), lambda i: (i, 0))
    full = lambda a: pl.BlockSpec(a.shape, lambda i: (0,) * a.ndim)
    return pl.pallas_call(
        _out_kernel,
        grid=(n // tm,),
        in_specs=[row(z2d.shape[1]), row(d), row(d), full(wg), full(w), full(gf)],
        out_specs=row(d),
        out_shape=jax.ShapeDtypeStruct((n, d), F32),
        compiler_params=_params(("parallel",)),
        name="hyena_out",
    )(z2d, hn2d, h2d, wg, w, gf)


def _rope_tables(seq):
    inv = 1.0 / (ROPE_THETA ** (jnp.arange(0, ROPE, 2, dtype=F32) / ROPE))
    ang = jnp.arange(seq, dtype=F32)[:, None] * inv[None, :]
    cos, sin = jnp.cos(ang), jnp.sin(ang)
    zero = jnp.zeros((seq, LANES - ROPE), F32)
    return (jnp.concatenate([cos, cos, zero], axis=1), jnp.concatenate([-sin, sin, zero], axis=1))


def _filter_tables(seq, width):
    t = jnp.linspace(0.0, 1.0, seq, dtype=F32)[:, None]
    w = 2.0 * math.pi * jnp.arange(seq, dtype=F32) / seq
    bands = jnp.linspace(1e-4, POS_BANDS - 1, POS_BANDS, dtype=F32)
    fw = w[:, None] * bands[None, :]
    z = jnp.concatenate([t, jnp.cos(fw), -jnp.sin(fw)], axis=-1)
    deltas = jnp.abs(jnp.linspace(MIN_DECAY, MAX_DECAY, width, dtype=F32))
    decay = jnp.exp(-t * deltas[None, :])
    idx = np.abs(np.arange(2 * seq) - seq)
    idx[0] = 0
    zcat = jnp.pad(z[idx].T, ((0, FILT_HIDDEN - POS_EMB), (0, 0)))
    deccat = decay[idx].T.at[:, 0].set(0.0)
    return zcat, deccat


def kernel(x, l0_norm, l0_w_in, l0_q_norm, l0_w_uq, l0_kv_norm, l0_w_ukv, l0_w_out, l1_norm, l1_w_in,
           l1_conv_w, l1_conv_b, l1_filt_w1, l1_filt_b1, l1_filt_w2, l1_filt_b2, l1_filt_w3, l1_filt_b3,
           l1_filt_w4, l1_filt_freq, l1_filt_bias, l1_w_out, final_norm):
    b, seq, d = x.shape
    width = l1_w_out.shape[0]
    n = b * seq
    tm = min(512, seq)
    tq = min(256, seq)
    row = lambda a: a.reshape(1, -1).astype(F32)
    col = lambda a: a.reshape(-1, 1).astype(F32)

    s1, s2, s3 = Q_RANK, Q_RANK + KV_RANK, Q_RANK + KV_RANK + ROPE
    half = ROPE // 2
    zpad = jnp.zeros((d, LANES - ROPE), F32)
    kpe_w = l0_w_in[:, s2:s3]
    win = jnp.concatenate([l0_w_in[:, :s2], kpe_w, zpad, kpe_w[:, half:], kpe_w[:, :half], zpad,
                           l0_w_in[:, s3:]], axis=1).astype(BF16)
    wuq3 = l0_w_uq.reshape(Q_RANK, HEADS, QK_DIM)
    wuq = jnp.pad(wuq3, ((0, 0), (0, 0), (0, 2 * LANES - QK_DIM))).reshape(Q_RANK, -1).astype(BF16)
    wuqs = jnp.concatenate([wuq3[:, :, NOPE + half:], wuq3[:, :, NOPE:NOPE + half],
                            jnp.zeros((Q_RANK, HEADS, LANES - ROPE), F32)], axis=2)
    wuqs = wuqs.reshape(Q_RANK, -1).astype(BF16)
    wukv3 = l0_w_ukv.reshape(KV_RANK, HEADS, NOPE + VDIM)
    wukv = jnp.concatenate([wukv3[:, :, :NOPE].reshape(KV_RANK, -1),
                            wukv3[:, :, NOPE:].reshape(KV_RANK, -1)], axis=1).astype(BF16)
    cos_t, sin_t = _rope_tables(seq)

    q, k, v, g = _mla_in(x.reshape(n, d), row(l0_norm), win, row(l0_q_norm), wuq, wuqs, row(l0_kv_norm),
                         wukv, cos_t, sin_t, seq, tm)
    h1, h1n = _attention(q.reshape(b, seq, -1), k.reshape(b, seq, -1), v.reshape(b, seq, -1),
                         g.reshape(b, seq, -1), x, l0_w_out.astype(BF16), row(l1_norm), tq)

    tn = 512
    nu = 3 * width
    cw = jnp.concatenate([l1_conv_w, l1_conv_b[None, :], jnp.zeros((SUBLANES - 4, nu), F32)],
                         axis=0).astype(F32)
    w_in1 = l1_w_in.astype(BF16)
    u = _hyena_in(h1n, w_in1[:, :nu], cw, tn, min(256, seq))
    nj = seq // LANES
    ut = u.reshape(b, nj, LANES, nu).transpose(3, 1, 0, 2)

    zcat, deccat = _filter_tables(seq, width)
    w4t = l1_filt_w4.T.reshape(2, 2, width, FILT_HIDDEN).astype(F32)
    w1t = jnp.pad(l1_filt_w1.T.astype(F32), ((0, 0), (0, FILT_HIDDEN - POS_EMB)))
    kf = _filters(zcat, w1t, col(l1_filt_b1), l1_filt_w2.T.astype(F32), col(l1_filt_b2),
                  l1_filt_w3.T.astype(F32), col(l1_filt_b3), col(l1_filt_freq), w4t, deccat,
                  min(256, width))
    kf = kf.reshape(2, width, 2 * seq // LANES, LANES)

    par = jnp.concatenate([l1_filt_bias.astype(F32), jnp.zeros((SUBLANES - 2, width), F32)], axis=0)
    par = jnp.broadcast_to(par.T[:, :, None], (width, SUBLANES, LANES))

    z2t = _long_conv(ut, kf, par, width, 16)
    z2 = z2t.transpose(2, 1, 3, 0).reshape(n, width)
    out = _hyena_out(z2, h1n.reshape(n, d), h1.reshape(n, d), w_in1[:, nu:], l1_w_out.astype(BF16),
                     row(final_norm), tm)
    return out.reshape(b, seq, d)
```

```python
import functools
import math

import jax
import jax.numpy as jnp
import numpy as np
from jax import lax
from jax.experimental import pallas as pl
from jax.experimental.pallas import tpu as pltpu

RMS_EPS = 1e-6
HEADS = 8
NOPE = 128
ROPE = 64
VDIM = 128
Q_RANK = 384
KV_RANK = 256
ROPE_THETA = 10000.0
QK_DIM = NOPE + ROPE

POS_EMB = 33
POS_BANDS = (POS_EMB - 1) // 2
FILT_HIDDEN = 64
MIN_DECAY = math.log(1e-2) / 1.5
MAX_DECAY = math.log(1e-2) / 0.3

LANES = 128
SUBLANES = 8
MXU_N = 256
VMEM_LIMIT = 56 * 1024 * 1024

F32 = jnp.float32
BF16 = jnp.bfloat16


def _rms(x, g):
    return x * lax.rsqrt(jnp.mean(x * x, axis=-1, keepdims=True) + RMS_EPS) * g


def _dot(a, b):
    return jnp.dot(a, b, preferred_element_type=F32)


def _params(sem):
    return pltpu.CompilerParams(dimension_semantics=sem, vmem_limit_bytes=VMEM_LIMIT)


def _mla_in_kernel(x_ref, g0_ref, win_ref, qn_ref, wuq_ref, wuqs_ref, kvn_ref, wukv_ref,
                   cos_ref, sin_ref, q_ref, k_ref, v_ref, g_ref):
    scale = QK_DIM ** -0.5
    xn = _rms(x_ref[...], g0_ref[...]).astype(BF16)
    proj = _dot(xn, win_ref[...])
    c_q = proj[:, :Q_RANK]
    c_kv = proj[:, Q_RANK:Q_RANK + KV_RANK]
    o = Q_RANK + KV_RANK
    kpe = proj[:, o:o + LANES]
    kpe_sw = proj[:, o + LANES:o + 2 * LANES]
    gate = proj[:, o + 2 * LANES:]
    cos = cos_ref[...]
    sin = sin_ref[...]

    cqn = _rms(c_q, qn_ref[...]).astype(BF16)
    qf = _dot(cqn, wuq_ref[...])
    qs = _dot(cqn, wuqs_ref[...])
    for h in range(HEADS):
        nope = qf[:, 2 * LANES * h:2 * LANES * h + LANES]
        pe = qf[:, 2 * LANES * h + LANES:2 * LANES * (h + 1)]
        pe = pe * cos + qs[:, LANES * h:LANES * (h + 1)] * sin
        q_ref[:, 2 * LANES * h:2 * LANES * h + LANES] = (nope * scale).astype(BF16)
        q_ref[:, 2 * LANES * h + LANES:2 * LANES * (h + 1)] = (pe * scale).astype(BF16)

    ckvn = _rms(c_kv, kvn_ref[...]).astype(BF16)
    kv = _dot(ckvn, wukv_ref[...])
    kpe_r = (kpe * cos + kpe_sw * sin).astype(BF16)
    for h in range(HEADS):
        k_ref[:, 2 * LANES * h:2 * LANES * h + LANES] = kv[:, LANES * h:LANES * (h + 1)].astype(BF16)
        k_ref[:, 2 * LANES * h + LANES:2 * LANES * (h + 1)] = kpe_r
    ones_col = (lax.broadcasted_iota(jnp.int32, (kv.shape[0], LANES), 1) == 0).astype(BF16)
    for h in range(HEADS):
        v_ref[:, 2 * LANES * h:2 * LANES * h + LANES] = kv[:, HEADS * NOPE + VDIM * h:
                                                           HEADS * NOPE + VDIM * (h + 1)].astype(BF16)
        v_ref[:, 2 * LANES * h + LANES:2 * LANES * (h + 1)] = ones_col
    g_ref[...] = (gate * jax.nn.sigmoid(gate)).astype(BF16)


def _mla_in(x2d, g0, win, qn, wuq, wuqs, kvn, wukv, cos_t, sin_t, seq, tm):
    n, d = x2d.shape
    nt = seq // tm
    full = lambda a: pl.BlockSpec(a.shape, lambda i: (0,) * a.ndim)
    return pl.pallas_call(
        _mla_in_kernel,
        grid=(n // tm,),
        in_specs=[pl.BlockSpec((tm, d), lambda i: (i, 0)), full(g0), full(win), full(qn), full(wuq),
                  full(wuqs), full(kvn), full(wukv),
                  pl.BlockSpec((tm, LANES), lambda i: (i % nt, 0)),
                  pl.BlockSpec((tm, LANES), lambda i: (i % nt, 0))],
        out_specs=[pl.BlockSpec((tm, HEADS * 2 * LANES), lambda i: (i, 0)),
                   pl.BlockSpec((tm, HEADS * 2 * LANES), lambda i: (i, 0)),
                   pl.BlockSpec((tm, HEADS * 2 * LANES), lambda i: (i, 0)),
                   pl.BlockSpec((tm, HEADS * VDIM), lambda i: (i, 0))],
        out_shape=[jax.ShapeDtypeStruct((n, HEADS * 2 * LANES), BF16),
                   jax.ShapeDtypeStruct((n, HEADS * 2 * LANES), BF16),
                   jax.ShapeDtypeStruct((n, HEADS * 2 * LANES), BF16),
                   jax.ShapeDtypeStruct((n, HEADS * VDIM), BF16)],
        compiler_params=_params(("parallel",)),
        name="mla_in",
    )(x2d, g0, win, qn, wuq, wuqs, kvn, wukv, cos_t, sin_t)


def _attn_kernel(q_ref, k_ref, v_ref, g_ref, x_ref, wout_ref, g1_ref, h_ref, hn_ref, o_ref):
    for h in range(HEADS):
        qh = q_ref[0, :, 2 * LANES * h:2 * LANES * (h + 1)]
        kh = k_ref[0, :, 2 * LANES * h:2 * LANES * (h + 1)]
        s = lax.dot_general(qh, kh, (((1,), (1,)), ((), ())), preferred_element_type=F32)
        m = jnp.max(s, axis=-1, keepdims=True)
        p = jnp.exp((s - m).astype(BF16))
        ov = _dot(p, v_ref[0, :, 2 * LANES * h:2 * LANES * (h + 1)])
        o_ref[:, VDIM * h:VDIM * (h + 1)] = ov[:, :VDIM] / ov[:, VDIM:VDIM + 1]
    og = (o_ref[...] * g_ref[0].astype(F32)).astype(BF16)
    h1 = x_ref[0] + _dot(og, wout_ref[...])
    h_ref[0] = h1
    hn_ref[0] = _rms(h1, g1_ref[...]).astype(BF16)


def _attention(q, k, v, g, x, wout, g1, tq):
    b, seq, d = x.shape
    qspec = lambda w: pl.BlockSpec((1, tq, w), lambda i, j: (i, j, 0))
    kspec = lambda w: pl.BlockSpec((1, seq, w), lambda i, j: (i, 0, 0))
    full = lambda a: pl.BlockSpec(a.shape, lambda i, j: (0,) * a.ndim)
    return pl.pallas_call(
        _attn_kernel,
        grid=(b, seq // tq),
        in_specs=[qspec(q.shape[-1]), kspec(k.shape[-1]), kspec(v.shape[-1]), qspec(g.shape[-1]),
                  qspec(d), full(wout), full(g1)],
        out_specs=[qspec(d), qspec(d)],
        out_shape=[jax.ShapeDtypeStruct((b, seq, d), F32), jax.ShapeDtypeStruct((b, seq, d), BF16)],
        scratch_shapes=[pltpu.VMEM((tq, HEADS * VDIM), F32)],
        compiler_params=_params(("parallel", "arbitrary")),
        name="mla_attn",
    )(q, k, v, g, x, wout, g1)


def _hyena_in_kernel(hn_ref, w_ref, cw_ref, o_ref, *, rc):
    seq, tn = hn_ref.shape[1], w_ref.shape[1]
    halo = 16
    w0, w1, w2, cb = cw_ref[0:1, :], cw_ref[1:2, :], cw_ref[2:3, :], cw_ref[3:4, :]
    rows = lax.broadcasted_iota(jnp.int32, (halo, tn), 0)
    nchunk = seq // rc
    for r in range(nchunk):
        lo, hi = max(rc * r - halo, 0), min(rc * (r + 1) + halo, seq)
        n = hi - lo
        proj = _dot(hn_ref[0, lo:hi, :], w_ref[...])
        prev = pltpu.roll(proj, 1, axis=0)
        nxt = pltpu.roll(proj, n - 1, axis=0)
        out = cb + w0 * prev + w1 * proj + w2 * nxt
        first = rc * r - lo
        o_ref[0, rc * r:rc * (r + 1), :] = out[first:first + rc].astype(BF16)
        if r == 0:
            top = cb + w0 * jnp.where(rows == 0, 0.0, prev[:halo]) + w1 * proj[:halo] + w2 * nxt[:halo]
            o_ref[0, :halo, :] = top.astype(BF16)
        if r == nchunk - 1:
            bot = (cb + w0 * prev[n - halo:] + w1 * proj[n - halo:]
                   + w2 * jnp.where(rows == halo - 1, 0.0, nxt[n - halo:]))
            o_ref[0, seq - halo:, :] = bot.astype(BF16)


def _hyena_in(hn, w, cw, tn, rc):
    b, seq, d = hn.shape
    nout = w.shape[1]
    return pl.pallas_call(
        functools.partial(_hyena_in_kernel, rc=rc),
        grid=(b, nout // tn),
        in_specs=[pl.BlockSpec((1, seq, d), lambda i, j: (i, 0, 0)),
                  pl.BlockSpec((d, tn), lambda i, j: (0, j)),
                  pl.BlockSpec((SUBLANES, tn), lambda i, j: (0, j))],
        out_specs=pl.BlockSpec((1, seq, tn), lambda i, j: (i, 0, j)),
        out_shape=jax.ShapeDtypeStruct((b, seq, nout), BF16),
        compiler_params=_params(("parallel", "arbitrary")),
        name="hyena_in",
    )(hn, w, cw)


def _bf16_bits(x):
    bits = lax.bitcast_convert_type(x, jnp.uint32)
    return (bits + jnp.uint32(0x7FFF) + ((bits >> 16) & jnp.uint32(1))) >> 16


def _filter_kernel(z_ref, w1_ref, b1_ref, w2_ref, b2_ref, w3_ref, b3_ref, fr_ref, w4_ref, dec_ref,
                   kf_ref, a_ref):
    seq = a_ref.shape[1] // 2
    hi = lax.Precision.HIGHEST

    @pl.when((pl.program_id(0) == 0) & (pl.program_id(1) == 0))
    def _():
        fr = fr_ref[...]
        a = jnp.sin(fr * (jnp.dot(w1_ref[...], z_ref[...], precision=hi, preferred_element_type=F32)
                          + b1_ref[...]))
        a = jnp.sin(fr * (jnp.dot(w2_ref[...], a, precision=hi, preferred_element_type=F32) + b2_ref[...]))
        a = jnp.sin(fr * (jnp.dot(w3_ref[...], a, precision=hi, preferred_element_type=F32) + b3_ref[...]))
        a_ref[...] = a

    bwd = jnp.dot(w4_ref[0, 1], a_ref[:, :seq], precision=hi, preferred_element_type=F32)
    fwd = jnp.dot(w4_ref[0, 0], a_ref[:, seq:], precision=hi, preferred_element_type=F32)
    kf = jnp.concatenate([bwd, fwd], axis=1) * dec_ref[...]
    kf_ref[0] = _bf16_bits(kf) | (_bf16_bits(pltpu.roll(kf, 1, axis=1)) << 16)


def _filters(zcat, w1t, b1, w2t, b2, w3t, b3, fr, w4t, deccat, tc):
    _, _, width, hid = w4t.shape
    two_l = zcat.shape[1]
    full = lambda a: pl.BlockSpec(a.shape, lambda n, i: (0,) * a.ndim)
    return pl.pallas_call(
        _filter_kernel,
        grid=(2, width // tc),
        in_specs=[full(zcat), full(w1t), full(b1), full(w2t), full(b2), full(w3t), full(b3), full(fr),
                  pl.BlockSpec((1, 2, tc, hid), lambda n, i: (n, 0, i, 0)),
                  pl.BlockSpec((tc, two_l), lambda n, i: (i, 0))],
        out_specs=pl.BlockSpec((1, tc, two_l), lambda n, i: (n, i, 0)),
        out_shape=jax.ShapeDtypeStruct((2, width, two_l), jnp.uint32),
        scratch_shapes=[pltpu.VMEM((hid, two_l), F32)],
        compiler_params=_params(("arbitrary", "arbitrary")),
        name="hyena_filters",
    )(zcat, w1t, b1, w2t, b2, w3t, b3, fr, w4t, deccat)


def _conv_kernel(x1_ref, x2_ref, v_ref, kf1_ref, kf2_ref, par_ref, o_ref,
                 lhs_ref, rhs1_ref, rhs2_ref, z1_ref, *, ct):
    nj, nb = v_ref.shape[1], v_ref.shape[2]
    ni = nj // 2
    ntile = kf1_ref.shape[1]
    ng = ntile - 2
    pad = 2 * ni - 2
    rows_per_word_tile = 2 * SUBLANES

    @pl.when(pl.program_id(0) == 0)
    def _():
        lhs_ref[...] = jnp.zeros(lhs_ref.shape, lhs_ref.dtype)

    diff = (lax.broadcasted_iota(jnp.int32, (SUBLANES, LANES), 1)
            - 2 * lax.broadcasted_iota(jnp.int32, (SUBLANES, LANES), 0))
    nroll = LANES // rows_per_word_tile

    def store_lhs(s, j, slab):
        for i in range(ni):
            jj = j + pad - 2 * i
            lhs_ref[s, nb * i:nb * (i + 1), LANES * jj:LANES * (jj + 1)] = slab

    def build_rhs(kf_ref, c, rhs_ref):
        prev = None
        for x in range(ntile - 1, -1, -1):
            tile = jnp.broadcast_to(kf_ref[c, x:x + 1, :], (SUBLANES, LANES))
            cur = [pltpu.roll(tile, rows_per_word_tile * e, axis=1, stride=2, stride_axis=0)
                   for e in range(nroll)]
            if prev is not None:
                g = ng - x
                for e in range(nroll):
                    words = jnp.where(diff < rows_per_word_tile * e, cur[e], prev[e])
                    blk = pltpu.bitcast(words, BF16)
                    r = LANES * g + rows_per_word_tile * e
                    if g < ng:
                        rhs_ref[r:r + rows_per_word_tile, LANES:2 * LANES] = blk
                    if g >= 1:
                        rhs_ref[r - LANES:r - LANES + rows_per_word_tile, 0:LANES] = blk
            prev = cur

    def piece(y, j):
        i, half = divmod(j, 2)
        return y[nb * i:nb * (i + 1), LANES * half:LANES * (half + 1)]

    slots = (0, 1)
    for s in slots:
        build_rhs(kf1_ref, s, rhs1_ref.at[s])

    def channel_pair(p, carry):
        cs = [2 * p + s for s in slots]
        nxt = [jnp.minimum(c + 2, ct - 2 + s) for s, c in zip(slots, cs)]
        bias1 = [par_ref[c, 0:1, :] for c in cs]
        bias2 = [par_ref[c, 1:2, :] for c in cs]
        for s, c in zip(slots, cs):
            for j in range(nj):
                store_lhs(s, j, v_ref[c, j])
        y1 = [_dot(lhs_ref[s], rhs1_ref[s]) for s in slots]
        for s, c in zip(slots, cs):
            build_rhs(kf2_ref, c, rhs2_ref.at[s])
        for s, c in zip(slots, cs):
            for j in range(nj):
                z1 = x1_ref[c, j].astype(F32) * (piece(y1[s], j) + bias1[s] * v_ref[c, j].astype(F32))
                z1_ref[s, j] = z1
                store_lhs(s, j, z1.astype(BF16))
        y2 = [_dot(lhs_ref[s], rhs2_ref[s]) for s in slots]
        for s in slots:
            build_rhs(kf1_ref, nxt[s], rhs1_ref.at[s])
        for s, c in zip(slots, cs):
            for j in range(nj):
                z2 = x2_ref[c, j].astype(F32) * (piece(y2[s], j) + bias2[s] * z1_ref[s, j])
                o_ref[c, j] = z2.astype(BF16)
        return carry

    lax.fori_loop(0, ct // 2, channel_pair, 0)


def _long_conv(ut, kf, par, first_channel, ct):
    _, nj, nb, _ = ut.shape
    wp = ut.shape[0] // 3
    seq = nj * LANES
    kdim = 2 * seq - MXU_N
    nblk = wp // ct
    base = first_channel // ct
    ntile = kf.shape[2]
    uspec = lambda o: pl.BlockSpec((ct, nj, nb, LANES), lambda i: (i + o * nblk, 0, 0, 0))
    kspec = lambda n: pl.BlockSpec((None, ct, ntile, LANES), lambda i: (n, i + base, 0, 0))
    return pl.pallas_call(
        functools.partial(_conv_kernel, ct=ct),
        grid=(nblk,),
        in_specs=[uspec(0), uspec(1), uspec(2), kspec(0), kspec(1),
                  pl.BlockSpec((ct, SUBLANES, LANES), lambda i: (i + base, 0, 0))],
        out_specs=pl.BlockSpec((ct, nj, nb, LANES), lambda i: (i, 0, 0, 0)),
        out_shape=jax.ShapeDtypeStruct((wp, nj, nb, LANES), BF16),
        scratch_shapes=[pltpu.VMEM((2, nj // 2 * nb, kdim), BF16),
                        pltpu.VMEM((2, kdim, MXU_N), BF16),
                        pltpu.VMEM((2, kdim, MXU_N), BF16),
                        pltpu.VMEM((2, nj, nb, LANES), F32)],
        compiler_params=_params(("arbitrary",)),
        name="hyena_long_conv",
    )(ut, ut, ut, kf, kf, par)


def _out_kernel(*refs):
    *z_refs, hn_ref, h_ref, wg_ref, w_ref, gf_ref, o_ref = refs
    gate = _dot(hn_ref[...], wg_ref[...])
    z = jnp.concatenate([r[...] for r in z_refs], axis=1).astype(F32)
    zg = (z * (gate * jax.nn.sigmoid(gate))).astype(BF16)
    h2 = h_ref[...] + _dot(zg, w_ref[...])
    o_ref[...] = _rms(h2, gf_ref[...])


def _hyena_out(z_parts, hn2d, h2d, wg, w, gf, tm):
    n, d = h2d.shape
    row = lambda w_: pl.BlockSpec((tm, w_), lambda i: (i, 0))
    full = lambda a: pl.BlockSpec(a.shape, lambda i: (0,) * a.ndim)
    return pl.pallas_call(
        _out_kernel,
        grid=(n // tm,),
        in_specs=[row(z.shape[1]) for z in z_parts] + [row(d), row(d), full(wg), full(w), full(gf)],
        out_specs=row(d),
        out_shape=jax.ShapeDtypeStruct((n, d), F32),
        compiler_params=_params(("parallel",)),
        name="hyena_out",
    )(*z_parts, hn2d, h2d, wg, w, gf)


def _rope_tables(seq):
    inv = 1.0 / (ROPE_THETA ** (jnp.arange(0, ROPE, 2, dtype=F32) / ROPE))
    ang = jnp.arange(seq, dtype=F32)[:, None] * inv[None, :]
    cos, sin = jnp.cos(ang), jnp.sin(ang)
    zero = jnp.zeros((seq, LANES - ROPE), F32)
    return (jnp.concatenate([cos, cos, zero], axis=1), jnp.concatenate([-sin, sin, zero], axis=1))


def _filter_tables(seq, width):
    t = jnp.linspace(0.0, 1.0, seq, dtype=F32)[:, None]
    w = 2.0 * math.pi * jnp.arange(seq, dtype=F32) / seq
    bands = jnp.linspace(1e-4, POS_BANDS - 1, POS_BANDS, dtype=F32)
    fw = w[:, None] * bands[None, :]
    z = jnp.concatenate([t, jnp.cos(fw), -jnp.sin(fw)], axis=-1)
    deltas = jnp.abs(jnp.linspace(MIN_DECAY, MAX_DECAY, width, dtype=F32))
    decay = jnp.exp(-t * deltas[None, :])
    idx = np.abs(np.arange(2 * seq) - seq)
    idx[0] = 0
    zcat = jnp.pad(z[idx].T, ((0, FILT_HIDDEN - POS_EMB), (0, 0)))
    deccat = decay[idx].T.at[:, 0].set(0.0)
    return zcat, deccat


def kernel(x, l0_norm, l0_w_in, l0_q_norm, l0_w_uq, l0_kv_norm, l0_w_ukv, l0_w_out, l1_norm, l1_w_in,
           l1_conv_w, l1_conv_b, l1_filt_w1, l1_filt_b1, l1_filt_w2, l1_filt_b2, l1_filt_w3, l1_filt_b3,
           l1_filt_w4, l1_filt_freq, l1_filt_bias, l1_w_out, final_norm):
    b, seq, d = x.shape
    width = l1_w_out.shape[0]
    n = b * seq
    tm = min(512, seq)
    tq = min(256, seq)
    row = lambda a: a.reshape(1, -1).astype(F32)
    col = lambda a: a.reshape(-1, 1).astype(F32)

    s1, s2, s3 = Q_RANK, Q_RANK + KV_RANK, Q_RANK + KV_RANK + ROPE
    half = ROPE // 2
    zpad = jnp.zeros((d, LANES - ROPE), F32)
    kpe_w = l0_w_in[:, s2:s3]
    win = jnp.concatenate([l0_w_in[:, :s2], kpe_w, zpad, kpe_w[:, half:], kpe_w[:, :half], zpad,
                           l0_w_in[:, s3:]], axis=1).astype(BF16)
    wuq3 = l0_w_uq.reshape(Q_RANK, HEADS, QK_DIM)
    wuq = jnp.pad(wuq3, ((0, 0), (0, 0), (0, 2 * LANES - QK_DIM))).reshape(Q_RANK, -1).astype(BF16)
    wuqs = jnp.concatenate([wuq3[:, :, NOPE + half:], wuq3[:, :, NOPE:NOPE + half],
                            jnp.zeros((Q_RANK, HEADS, LANES - ROPE), F32)], axis=2)
    wuqs = wuqs.reshape(Q_RANK, -1).astype(BF16)
    wukv3 = l0_w_ukv.reshape(KV_RANK, HEADS, NOPE + VDIM)
    wukv = jnp.concatenate([wukv3[:, :, :NOPE].reshape(KV_RANK, -1),
                            wukv3[:, :, NOPE:].reshape(KV_RANK, -1)], axis=1).astype(BF16)
    cos_t, sin_t = _rope_tables(seq)

    q, k, v, g = _mla_in(x.reshape(n, d), row(l0_norm), win, row(l0_q_norm), wuq, wuqs, row(l0_kv_norm),
                         wukv, cos_t, sin_t, seq, tm)
    h1, h1n = _attention(q.reshape(b, seq, -1), k.reshape(b, seq, -1), v.reshape(b, seq, -1),
                         g.reshape(b, seq, -1), x, l0_w_out.astype(BF16), row(l1_norm), tq)

    nu = 3 * width
    nj = seq // LANES
    cw = jnp.concatenate([l1_conv_w, l1_conv_b[None, :], jnp.zeros((SUBLANES - 4, nu), F32)],
                         axis=0).astype(F32)
    w_in1 = l1_w_in.astype(BF16)
    parts = 2
    wp = width // parts
    uts = []
    for part in range(parts):
        cols = jnp.concatenate([jnp.arange(wp) + grp * width + part * wp for grp in range(3)])
        u = _hyena_in(h1n, w_in1[:, cols], cw[:, cols], wp, min(512, seq))
        uts.append(u.reshape(b, nj, LANES, 3 * wp).transpose(3, 1, 0, 2))

    zcat, deccat = _filter_tables(seq, width)
    w4t = l1_filt_w4.T.reshape(2, 2, width, FILT_HIDDEN).astype(F32)
    w1t = jnp.pad(l1_filt_w1.T.astype(F32), ((0, 0), (0, FILT_HIDDEN - POS_EMB)))
    kf = _filters(zcat, w1t, col(l1_filt_b1), l1_filt_w2.T.astype(F32), col(l1_filt_b2),
                  l1_filt_w3.T.astype(F32), col(l1_filt_b3), col(l1_filt_freq), w4t, deccat,
                  min(256, width))
    kf = kf.reshape(2, width, 2 * seq // LANES, LANES)

    par = jnp.concatenate([l1_filt_bias.astype(F32), jnp.zeros((SUBLANES - 2, width), F32)], axis=0)
    par = jnp.broadcast_to(par.T[:, :, None], (width, SUBLANES, LANES))

    z_parts = []
    for part in range(parts):
        z2t = _long_conv(uts[part], kf, par, part * wp, 16)
        z_parts.append(z2t.transpose(2, 1, 3, 0).reshape(n, wp))
    out = _hyena_out(z_parts, h1n.reshape(n, d), h1.reshape(n, d), w_in1[:, nu:], l1_w_out.astype(BF16),
                     row(final_norm), tm)
    return out.reshape(b, seq, d)
```

```python
import functools
import math

import jax
import jax.numpy as jnp
import numpy as np
from jax import lax
from jax.experimental import pallas as pl
from jax.experimental.pallas import tpu as pltpu

RMS_EPS = 1e-6
HEADS = 8
NOPE = 128
ROPE = 64
VDIM = 128
Q_RANK = 384
KV_RANK = 256
ROPE_THETA = 10000.0
QK_DIM = NOPE + ROPE

POS_EMB = 33
POS_BANDS = (POS_EMB - 1) // 2
FILT_HIDDEN = 64
MIN_DECAY = math.log(1e-2) / 1.5
MAX_DECAY = math.log(1e-2) / 0.3

LANES = 128
SUBLANES = 8
MXU_N = 256
VMEM_LIMIT = 56 * 1024 * 1024

F32 = jnp.float32
BF16 = jnp.bfloat16


def _rms(x, g):
    return x * lax.rsqrt(jnp.mean(x * x, axis=-1, keepdims=True) + RMS_EPS) * g


def _dot(a, b):
    return jnp.dot(a, b, preferred_element_type=F32)


def _params(sem):
    return pltpu.CompilerParams(dimension_semantics=sem, vmem_limit_bytes=VMEM_LIMIT)


def _mla_in_kernel(x_ref, g0_ref, win_ref, qn_ref, wuq_ref, kvn_ref, wukv_ref,
                   cos_ref, sin_ref, q_ref, k_ref, v_ref, g_ref):
    scale = QK_DIM ** -0.5
    xn = _rms(x_ref[...], g0_ref[...]).astype(BF16)
    proj = _dot(xn, win_ref[...])
    c_q = proj[:, :Q_RANK]
    c_kv = proj[:, Q_RANK:Q_RANK + KV_RANK]
    o = Q_RANK + KV_RANK
    kpe = proj[:, o:o + LANES]
    gate = proj[:, o + LANES:]
    cos = cos_ref[...]
    sin = sin_ref[...]

    def rope(pe):
        return pe * cos + pltpu.roll(pe, ROPE // 2, axis=1) * sin

    cqn = _rms(c_q, qn_ref[...]).astype(BF16)
    qf = _dot(cqn, wuq_ref[...])
    for h in range(HEADS):
        nope = qf[:, 2 * LANES * h:2 * LANES * h + LANES]
        pe = rope(qf[:, 2 * LANES * h + LANES:2 * LANES * (h + 1)])
        q_ref[:, 2 * LANES * h:2 * LANES * h + LANES] = (nope * scale).astype(BF16)
        q_ref[:, 2 * LANES * h + LANES:2 * LANES * (h + 1)] = (pe * scale).astype(BF16)

    ckvn = _rms(c_kv, kvn_ref[...]).astype(BF16)
    kv = _dot(ckvn, wukv_ref[...])
    kpe_r = rope(kpe).astype(BF16)
    for h in range(HEADS):
        k_ref[:, 2 * LANES * h:2 * LANES * h + LANES] = kv[:, LANES * h:LANES * (h + 1)].astype(BF16)
        k_ref[:, 2 * LANES * h + LANES:2 * LANES * (h + 1)] = kpe_r
    ones_col = (lax.broadcasted_iota(jnp.int32, (kv.shape[0], LANES), 1) == 0).astype(BF16)
    for h in range(HEADS):
        v_ref[:, 2 * LANES * h:2 * LANES * h + LANES] = kv[:, HEADS * NOPE + VDIM * h:
                                                           HEADS * NOPE + VDIM * (h + 1)].astype(BF16)
        v_ref[:, 2 * LANES * h + LANES:2 * LANES * (h + 1)] = ones_col
    g_ref[...] = (gate * jax.nn.sigmoid(gate)).astype(BF16)


def _mla_in(x2d, g0, win, qn, wuq, kvn, wukv, cos_t, sin_t, seq, tm):
    n, d = x2d.shape
    nt = seq // tm
    full = lambda a: pl.BlockSpec(a.shape, lambda i: (0,) * a.ndim)
    return pl.pallas_call(
        _mla_in_kernel,
        grid=(n // tm,),
        in_specs=[pl.BlockSpec((tm, d), lambda i: (i, 0)), full(g0), full(win), full(qn), full(wuq),
                  full(kvn), full(wukv),
                  pl.BlockSpec((tm, LANES), lambda i: (i % nt, 0)),
                  pl.BlockSpec((tm, LANES), lambda i: (i % nt, 0))],
        out_specs=[pl.BlockSpec((tm, HEADS * 2 * LANES), lambda i: (i, 0)),
                   pl.BlockSpec((tm, HEADS * 2 * LANES), lambda i: (i, 0)),
                   pl.BlockSpec((tm, HEADS * 2 * LANES), lambda i: (i, 0)),
                   pl.BlockSpec((tm, HEADS * VDIM), lambda i: (i, 0))],
        out_shape=[jax.ShapeDtypeStruct((n, HEADS * 2 * LANES), BF16),
                   jax.ShapeDtypeStruct((n, HEADS * 2 * LANES), BF16),
                   jax.ShapeDtypeStruct((n, HEADS * 2 * LANES), BF16),
                   jax.ShapeDtypeStruct((n, HEADS * VDIM), BF16)],
        compiler_params=_params(("parallel",)),
        name="mla_in",
    )(x2d, g0, win, qn, wuq, kvn, wukv, cos_t, sin_t)


def _attn_kernel(q_ref, k_ref, v_ref, g_ref, x_ref, wout_ref, g1_ref, h_ref, hn_ref, o_ref):
    for h in range(HEADS):
        qh = q_ref[0, :, 2 * LANES * h:2 * LANES * (h + 1)]
        kh = k_ref[0, :, 2 * LANES * h:2 * LANES * (h + 1)]
        s = lax.dot_general(qh, kh, (((1,), (1,)), ((), ())), preferred_element_type=F32)
        m = jnp.max(s, axis=-1, keepdims=True)
        p = jnp.exp((s - m).astype(BF16))
        ov = _dot(p, v_ref[0, :, 2 * LANES * h:2 * LANES * (h + 1)])
        o_ref[:, VDIM * h:VDIM * (h + 1)] = ov[:, :VDIM] / ov[:, VDIM:VDIM + 1]
    og = (o_ref[...] * g_ref[0].astype(F32)).astype(BF16)
    h1 = x_ref[0] + _dot(og, wout_ref[...])
    h_ref[0] = h1
    hn_ref[0] = _rms(h1, g1_ref[...]).astype(BF16)


def _attention(q, k, v, g, x, wout, g1, tq):
    b, seq, d = x.shape
    qspec = lambda w: pl.BlockSpec((1, tq, w), lambda i, j: (i, j, 0))
    kspec = lambda w: pl.BlockSpec((1, seq, w), lambda i, j: (i, 0, 0))
    full = lambda a: pl.BlockSpec(a.shape, lambda i, j: (0,) * a.ndim)
    return pl.pallas_call(
        _attn_kernel,
        grid=(b, seq // tq),
        in_specs=[qspec(q.shape[-1]), kspec(k.shape[-1]), kspec(v.shape[-1]), qspec(g.shape[-1]),
                  qspec(d), full(wout), full(g1)],
        out_specs=[qspec(d), qspec(d)],
        out_shape=[jax.ShapeDtypeStruct((b, seq, d), F32), jax.ShapeDtypeStruct((b, seq, d), BF16)],
        scratch_shapes=[pltpu.VMEM((tq, HEADS * VDIM), F32)],
        compiler_params=_params(("parallel", "arbitrary")),
        name="mla_attn",
    )(q, k, v, g, x, wout, g1)


def _hyena_in_kernel(hn_ref, w_ref, cw_ref, o_ref, *, rc):
    seq, tn = hn_ref.shape[1], w_ref.shape[1]
    halo = 16
    w0, w1, w2, cb = cw_ref[0:1, :], cw_ref[1:2, :], cw_ref[2:3, :], cw_ref[3:4, :]
    rows = lax.broadcasted_iota(jnp.int32, (halo, tn), 0)
    nchunk = seq // rc
    for r in range(nchunk):
        lo, hi = max(rc * r - halo, 0), min(rc * (r + 1) + halo, seq)
        n = hi - lo
        proj = _dot(hn_ref[0, lo:hi, :], w_ref[...])
        prev = pltpu.roll(proj, 1, axis=0)
        nxt = pltpu.roll(proj, n - 1, axis=0)
        out = cb + w0 * prev + w1 * proj + w2 * nxt
        first = rc * r - lo
        o_ref[0, rc * r:rc * (r + 1), :] = out[first:first + rc].astype(BF16)
        if r == 0:
            top = cb + w0 * jnp.where(rows == 0, 0.0, prev[:halo]) + w1 * proj[:halo] + w2 * nxt[:halo]
            o_ref[0, :halo, :] = top.astype(BF16)
        if r == nchunk - 1:
            bot = (cb + w0 * prev[n - halo:] + w1 * proj[n - halo:]
                   + w2 * jnp.where(rows == halo - 1, 0.0, nxt[n - halo:]))
            o_ref[0, seq - halo:, :] = bot.astype(BF16)


def _hyena_in(hn, w, cw, tn, rc):
    b, seq, d = hn.shape
    nout = w.shape[1]
    return pl.pallas_call(
        functools.partial(_hyena_in_kernel, rc=rc),
        grid=(b, nout // tn),
        in_specs=[pl.BlockSpec((1, seq, d), lambda i, j: (i, 0, 0)),
                  pl.BlockSpec((d, tn), lambda i, j: (0, j)),
                  pl.BlockSpec((SUBLANES, tn), lambda i, j: (0, j))],
        out_specs=pl.BlockSpec((1, seq, tn), lambda i, j: (i, 0, j)),
        out_shape=jax.ShapeDtypeStruct((b, seq, nout), BF16),
        compiler_params=_params(("parallel", "arbitrary")),
        name="hyena_in",
    )(hn, w, cw)


def _bf16_bits(x):
    bits = lax.bitcast_convert_type(x, jnp.uint32)
    return (bits + jnp.uint32(0x7FFF) + ((bits >> 16) & jnp.uint32(1))) >> 16


def _filter_kernel(z_ref, w1_ref, b1_ref, w2_ref, b2_ref, w3_ref, b3_ref, fr_ref, w4_ref, dec_ref,
                   kf_ref, a_ref):
    seq = a_ref.shape[1] // 2
    hi = lax.Precision.HIGHEST

    @pl.when((pl.program_id(0) == 0) & (pl.program_id(1) == 0))
    def _():
        fr = fr_ref[...]
        a = jnp.sin(fr * (jnp.dot(w1_ref[...], z_ref[...], precision=hi, preferred_element_type=F32)
                          + b1_ref[...]))
        a = jnp.sin(fr * (jnp.dot(w2_ref[...], a, precision=hi, preferred_element_type=F32) + b2_ref[...]))
        a = jnp.sin(fr * (jnp.dot(w3_ref[...], a, precision=hi, preferred_element_type=F32) + b3_ref[...]))
        a_ref[...] = a

    bwd = jnp.dot(w4_ref[0, 1], a_ref[:, :seq], precision=hi, preferred_element_type=F32)
    fwd = jnp.dot(w4_ref[0, 0], a_ref[:, seq:], precision=hi, preferred_element_type=F32)
    kf = jnp.concatenate([bwd, fwd], axis=1) * dec_ref[...]
    kf_ref[0] = _bf16_bits(kf) | (_bf16_bits(pltpu.roll(kf, 1, axis=1)) << 16)


def _filters(zcat, w1t, b1, w2t, b2, w3t, b3, fr, w4t, deccat, tc):
    _, _, width, hid = w4t.shape
    two_l = zcat.shape[1]
    full = lambda a: pl.BlockSpec(a.shape, lambda n, i: (0,) * a.ndim)
    return pl.pallas_call(
        _filter_kernel,
        grid=(2, width // tc),
        in_specs=[full(zcat), full(w1t), full(b1), full(w2t), full(b2), full(w3t), full(b3), full(fr),
                  pl.BlockSpec((1, 2, tc, hid), lambda n, i: (n, 0, i, 0)),
                  pl.BlockSpec((tc, two_l), lambda n, i: (i, 0))],
        out_specs=pl.BlockSpec((1, tc, two_l), lambda n, i: (n, i, 0)),
        out_shape=jax.ShapeDtypeStruct((2, width, two_l), jnp.uint32),
        scratch_shapes=[pltpu.VMEM((hid, two_l), F32)],
        compiler_params=_params(("arbitrary", "arbitrary")),
        name="hyena_filters",
    )(zcat, w1t, b1, w2t, b2, w3t, b3, fr, w4t, deccat)


def _conv_kernel(x1_ref, x2_ref, v_ref, kf1_ref, kf2_ref, par_ref, o_ref,
                 lhs_ref, rhs1_ref, rhs2_ref, z1_ref, *, ct):
    nj, nb = v_ref.shape[1], v_ref.shape[2]
    ni = nj // 2
    ntile = kf1_ref.shape[1]
    ng = ntile - 2
    pad = 2 * ni - 2
    rows_per_word_tile = 2 * SUBLANES

    @pl.when(pl.program_id(0) == 0)
    def _():
        lhs_ref[...] = jnp.zeros(lhs_ref.shape, lhs_ref.dtype)

    diff = (lax.broadcasted_iota(jnp.int32, (SUBLANES, LANES), 1)
            - 2 * lax.broadcasted_iota(jnp.int32, (SUBLANES, LANES), 0))
    nroll = LANES // rows_per_word_tile

    def store_lhs(s, j, slab):
        for i in range(ni):
            jj = j + pad - 2 * i
            lhs_ref[s, nb * i:nb * (i + 1), LANES * jj:LANES * (jj + 1)] = slab

    def build_rhs(kf_ref, c, rhs_ref):
        prev = None
        for x in range(ntile - 1, -1, -1):
            tile = jnp.broadcast_to(kf_ref[c, x:x + 1, :], (SUBLANES, LANES))
            cur = [pltpu.roll(tile, rows_per_word_tile * e, axis=1, stride=2, stride_axis=0)
                   for e in range(nroll)]
            if prev is not None:
                g = ng - x
                for e in range(nroll):
                    words = jnp.where(diff < rows_per_word_tile * e, cur[e], prev[e])
                    blk = pltpu.bitcast(words, BF16)
                    r = LANES * g + rows_per_word_tile * e
                    if g < ng:
                        rhs_ref[r:r + rows_per_word_tile, LANES:2 * LANES] = blk
                    if g >= 1:
                        rhs_ref[r - LANES:r - LANES + rows_per_word_tile, 0:LANES] = blk
            prev = cur

    def piece(y, j):
        i, half = divmod(j, 2)
        return y[nb * i:nb * (i + 1), LANES * half:LANES * (half + 1)]

    slots = (0, 1)
    for s in slots:
        build_rhs(kf1_ref, s, rhs1_ref.at[s])

    def channel_pair(p, carry):
        cs = [2 * p + s for s in slots]
        nxt = [jnp.minimum(c + 2, ct - 2 + s) for s, c in zip(slots, cs)]
        bias1 = [par_ref[c, 0:1, :] for c in cs]
        bias2 = [par_ref[c, 1:2, :] for c in cs]
        for s, c in zip(slots, cs):
            for j in range(nj):
                store_lhs(s, j, v_ref[c, j])
        y1 = [_dot(lhs_ref[s], rhs1_ref[s]) for s in slots]
        for s, c in zip(slots, cs):
            build_rhs(kf2_ref, c, rhs2_ref.at[s])
        for s, c in zip(slots, cs):
            for j in range(nj):
                z1 = x1_ref[c, j].astype(F32) * (piece(y1[s], j) + bias1[s] * v_ref[c, j].astype(F32))
                z1_ref[s, j] = z1
                store_lhs(s, j, z1.astype(BF16))
        y2 = [_dot(lhs_ref[s], rhs2_ref[s]) for s in slots]
        for s in slots:
            build_rhs(kf1_ref, nxt[s], rhs1_ref.at[s])
        for s, c in zip(slots, cs):
            for j in range(nj):
                z2 = x2_ref[c, j].astype(F32) * (piece(y2[s], j) + bias2[s] * z1_ref[s, j])
                o_ref[c, j] = z2.astype(BF16)
        return carry

    lax.fori_loop(0, ct // 2, channel_pair, 0)


def _long_conv(ut, kf, par, first_channel, ct):
    _, nj, nb, _ = ut.shape
    wp = ut.shape[0] // 3
    seq = nj * LANES
    kdim = 2 * seq - MXU_N
    nblk = wp // ct
    base = first_channel // ct
    ntile = kf.shape[2]
    uspec = lambda o: pl.BlockSpec((ct, nj, nb, LANES), lambda i: (i + o * nblk, 0, 0, 0))
    kspec = lambda n: pl.BlockSpec((None, ct, ntile, LANES), lambda i: (n, i + base, 0, 0))
    return pl.pallas_call(
        functools.partial(_conv_kernel, ct=ct),
        grid=(nblk,),
        in_specs=[uspec(0), uspec(1), uspec(2), kspec(0), kspec(1),
                  pl.BlockSpec((ct, SUBLANES, LANES), lambda i: (i + base, 0, 0))],
        out_specs=pl.BlockSpec((ct, nj, nb, LANES), lambda i: (i, 0, 0, 0)),
        out_shape=jax.ShapeDtypeStruct((wp, nj, nb, LANES), BF16),
        scratch_shapes=[pltpu.VMEM((2, nj // 2 * nb, kdim), BF16),
                        pltpu.VMEM((2, kdim, MXU_N), BF16),
                        pltpu.VMEM((2, kdim, MXU_N), BF16),
                        pltpu.VMEM((2, nj, nb, LANES), F32)],
        compiler_params=_params(("arbitrary",)),
        name="hyena_long_conv",
    )(ut, ut, ut, kf, kf, par)


def _out_kernel(*refs):
    *z_refs, hn_ref, h_ref, wg_ref, w_ref, gf_ref, o_ref = refs
    gate = _dot(hn_ref[...], wg_ref[...])
    z = jnp.concatenate([r[...] for r in z_refs], axis=1).astype(F32)
    zg = (z * (gate * jax.nn.sigmoid(gate))).astype(BF16)
    h2 = h_ref[...] + _dot(zg, w_ref[...])
    o_ref[...] = _rms(h2, gf_ref[...])


def _hyena_out(z_parts, hn2d, h2d, wg, w, gf, tm):
    n, d = h2d.shape
    row = lambda w_: pl.BlockSpec((tm, w_), lambda i: (i, 0))
    full = lambda a: pl.BlockSpec(a.shape, lambda i: (0,) * a.ndim)
    return pl.pallas_call(
        _out_kernel,
        grid=(n // tm,),
        in_specs=[row(z.shape[1]) for z in z_parts] + [row(d), row(d), full(wg), full(w), full(gf)],
        out_specs=row(d),
        out_shape=jax.ShapeDtypeStruct((n, d), F32),
        compiler_params=_params(("parallel",)),
        name="hyena_out",
    )(*z_parts, hn2d, h2d, wg, w, gf)


def _rope_tables(seq):
    inv = 1.0 / (ROPE_THETA ** (jnp.arange(0, ROPE, 2, dtype=F32) / ROPE))
    ang = jnp.arange(seq, dtype=F32)[:, None] * inv[None, :]
    cos, sin = jnp.cos(ang), jnp.sin(ang)
    zero = jnp.zeros((seq, LANES - ROPE), F32)
    return (jnp.concatenate([cos, cos, zero], axis=1), jnp.concatenate([-sin, sin, zero], axis=1))


def _filter_tables(seq, width):
    t = jnp.linspace(0.0, 1.0, seq, dtype=F32)[:, None]
    w = 2.0 * math.pi * jnp.arange(seq, dtype=F32) / seq
    bands = jnp.linspace(1e-4, POS_BANDS - 1, POS_BANDS, dtype=F32)
    fw = w[:, None] * bands[None, :]
    z = jnp.concatenate([t, jnp.cos(fw), -jnp.sin(fw)], axis=-1)
    deltas = jnp.abs(jnp.linspace(MIN_DECAY, MAX_DECAY, width, dtype=F32))
    decay = jnp.exp(-t * deltas[None, :])
    idx = np.abs(np.arange(2 * seq) - seq)
    idx[0] = 0
    zcat = jnp.pad(z[idx].T, ((0, FILT_HIDDEN - POS_EMB), (0, 0)))
    deccat = decay[idx].T.at[:, 0].set(0.0)
    return zcat, deccat


def kernel(x, l0_norm, l0_w_in, l0_q_norm, l0_w_uq, l0_kv_norm, l0_w_ukv, l0_w_out, l1_norm, l1_w_in,
           l1_conv_w, l1_conv_b, l1_filt_w1, l1_filt_b1, l1_filt_w2, l1_filt_b2, l1_filt_w3, l1_filt_b3,
           l1_filt_w4, l1_filt_freq, l1_filt_bias, l1_w_out, final_norm):
    b, seq, d = x.shape
    width = l1_w_out.shape[0]
    n = b * seq
    tm = min(512, seq)
    tq = min(256, seq)
    row = lambda a: a.reshape(1, -1).astype(F32)
    col = lambda a: a.reshape(-1, 1).astype(F32)

    s1, s2, s3 = Q_RANK, Q_RANK + KV_RANK, Q_RANK + KV_RANK + ROPE
    kpe_w = l0_w_in[:, s2:s3]
    win = jnp.concatenate([l0_w_in[:, :s2], kpe_w, kpe_w, l0_w_in[:, s3:]], axis=1).astype(BF16)
    wuq3 = l0_w_uq.reshape(Q_RANK, HEADS, QK_DIM)
    wuq = jnp.concatenate([wuq3, wuq3[:, :, NOPE:]], axis=2).reshape(Q_RANK, -1).astype(BF16)
    wukv3 = l0_w_ukv.reshape(KV_RANK, HEADS, NOPE + VDIM)
    wukv = jnp.concatenate([wukv3[:, :, :NOPE].reshape(KV_RANK, -1),
                            wukv3[:, :, NOPE:].reshape(KV_RANK, -1)], axis=1).astype(BF16)
    cos_t, sin_t = _rope_tables(seq)

    q, k, v, g = _mla_in(x.reshape(n, d), row(l0_norm), win, row(l0_q_norm), wuq, row(l0_kv_norm),
                         wukv, cos_t, sin_t, seq, tm)
    h1, h1n = _attention(q.reshape(b, seq, -1), k.reshape(b, seq, -1), v.reshape(b, seq, -1),
                         g.reshape(b, seq, -1), x, l0_w_out.astype(BF16), row(l1_norm), tq)

    nu = 3 * width
    nj = seq // LANES
    cw = jnp.concatenate([l1_conv_w, l1_conv_b[None, :], jnp.zeros((SUBLANES - 4, nu), F32)],
                         axis=0).astype(F32)
    w_in1 = l1_w_in.astype(BF16)
    parts = 2
    wp = width // parts
    uts = []
    for part in range(parts):
        take = lambda a: jnp.concatenate([a[:, grp * width + part * wp:grp * width + (part + 1) * wp]
                                          for grp in range(3)], axis=1)
        u = _hyena_in(h1n, take(w_in1), take(cw), wp, min(512, seq))
        uts.append(u.reshape(b, nj, LANES, 3 * wp).transpose(3, 1, 0, 2))

    zcat, deccat = _filter_tables(seq, width)
    w4t = l1_filt_w4.T.reshape(2, 2, width, FILT_HIDDEN).astype(F32)
    w1t = jnp.pad(l1_filt_w1.T.astype(F32), ((0, 0), (0, FILT_HIDDEN - POS_EMB)))
    kf = _filters(zcat, w1t, col(l1_filt_b1), l1_filt_w2.T.astype(F32), col(l1_filt_b2),
                  l1_filt_w3.T.astype(F32), col(l1_filt_b3), col(l1_filt_freq), w4t, deccat,
                  min(256, width))
    kf = kf.reshape(2, width, 2 * seq // LANES, LANES)

    par = jnp.concatenate([l1_filt_bias.astype(F32), jnp.zeros((SUBLANES - 2, width), F32)], axis=0)
    par = jnp.broadcast_to(par.T[:, :, None], (width, SUBLANES, LANES))

    z_parts = []
    for part in range(parts):
        z2t = _long_conv(uts[part], kf, par, part * wp, 32)
        z_parts.append(z2t.transpose(2, 1, 3, 0).reshape(n, wp))
    out = _hyena_out(z_parts, h1n.reshape(n, d), h1.reshape(n, d), w_in1[:, nu:], l1_w_out.astype(BF16),
                     row(final_norm), tm)
    return out.reshape(b, seq, d)
```

```python
import functools
import math

import jax
import jax.numpy as jnp
import numpy as np
from jax import lax
from jax.experimental import pallas as pl
from jax.experimental.pallas import tpu as pltpu

RMS_EPS = 1e-6
HEADS = 8
NOPE = 128
ROPE = 64
VDIM = 128
Q_RANK = 384
KV_RANK = 256
ROPE_THETA = 10000.0
QK_DIM = NOPE + ROPE

POS_EMB = 33
POS_BANDS = (POS_EMB - 1) // 2
FILT_HIDDEN = 64
MIN_DECAY = math.log(1e-2) / 1.5
MAX_DECAY = math.log(1e-2) / 0.3

LANES = 128
SUBLANES = 8
MXU_N = 256
VMEM_LIMIT = 60 * 1024 * 1024

F32 = jnp.float32
BF16 = jnp.bfloat16


def _rms(x, g):
    return x * lax.rsqrt(jnp.mean(x * x, axis=-1, keepdims=True) + RMS_EPS) * g


def _dot(a, b):
    return jnp.dot(a, b, preferred_element_type=F32)


def _params(sem):
    return pltpu.CompilerParams(dimension_semantics=sem, vmem_limit_bytes=VMEM_LIMIT)


def _mla_in_kernel(x_ref, g0_ref, win_ref, qn_ref, wuq_ref, kvn_ref, wukv_ref,
                   cos_ref, sin_ref, q_ref, k_ref, v_ref, g_ref):
    scale = QK_DIM ** -0.5 * math.log2(math.e)
    xn = _rms(x_ref[...], g0_ref[...]).astype(BF16)
    proj = _dot(xn, win_ref[...])
    c_q = proj[:, :Q_RANK]
    c_kv = proj[:, Q_RANK:Q_RANK + KV_RANK]
    o = Q_RANK + KV_RANK
    kpe = proj[:, o:o + LANES]
    gate = proj[:, o + LANES:]
    cos = cos_ref[...]
    sin = sin_ref[...]

    def rope(pe):
        return pe * cos + pltpu.roll(pe, ROPE // 2, axis=1) * sin

    cqn = _rms(c_q, qn_ref[...]).astype(BF16)
    qf = _dot(cqn, wuq_ref[...])
    for h in range(HEADS):
        nope = qf[:, 2 * LANES * h:2 * LANES * h + LANES]
        pe = rope(qf[:, 2 * LANES * h + LANES:2 * LANES * (h + 1)])
        q_ref[:, 2 * LANES * h:2 * LANES * h + LANES] = (nope * scale).astype(BF16)
        q_ref[:, 2 * LANES * h + LANES:2 * LANES * (h + 1)] = (pe * scale).astype(BF16)

    ckvn = _rms(c_kv, kvn_ref[...]).astype(BF16)
    kv = _dot(ckvn, wukv_ref[...])
    kpe_r = rope(kpe).astype(BF16)
    for h in range(HEADS):
        k_ref[:, 2 * LANES * h:2 * LANES * h + LANES] = kv[:, LANES * h:LANES * (h + 1)].astype(BF16)
        k_ref[:, 2 * LANES * h + LANES:2 * LANES * (h + 1)] = kpe_r
    ones_col = (lax.broadcasted_iota(jnp.int32, (kv.shape[0], LANES), 1) == 0).astype(BF16)
    for h in range(HEADS):
        v_ref[:, 2 * LANES * h:2 * LANES * h + LANES] = kv[:, HEADS * NOPE + VDIM * h:
                                                           HEADS * NOPE + VDIM * (h + 1)].astype(BF16)
        v_ref[:, 2 * LANES * h + LANES:2 * LANES * (h + 1)] = ones_col
    g_ref[...] = (gate * jax.nn.sigmoid(gate)).astype(BF16)


def _mla_in(x2d, g0, win, qn, wuq, kvn, wukv, cos_t, sin_t, seq, tm):
    n, d = x2d.shape
    nt = seq // tm
    full = lambda a: pl.BlockSpec(a.shape, lambda i: (0,) * a.ndim)
    return pl.pallas_call(
        _mla_in_kernel,
        grid=(n // tm,),
        in_specs=[pl.BlockSpec((tm, d), lambda i: (i, 0)), full(g0), full(win), full(qn), full(wuq),
                  full(kvn), full(wukv),
                  pl.BlockSpec((tm, LANES), lambda i: (i % nt, 0)),
                  pl.BlockSpec((tm, LANES), lambda i: (i % nt, 0))],
        out_specs=[pl.BlockSpec((tm, HEADS * 2 * LANES), lambda i: (i, 0)),
                   pl.BlockSpec((tm, HEADS * 2 * LANES), lambda i: (i, 0)),
                   pl.BlockSpec((tm, HEADS * 2 * LANES), lambda i: (i, 0)),
                   pl.BlockSpec((tm, HEADS * VDIM), lambda i: (i, 0))],
        out_shape=[jax.ShapeDtypeStruct((n, HEADS * 2 * LANES), BF16),
                   jax.ShapeDtypeStruct((n, HEADS * 2 * LANES), BF16),
                   jax.ShapeDtypeStruct((n, HEADS * 2 * LANES), BF16),
                   jax.ShapeDtypeStruct((n, HEADS * VDIM), BF16)],
        compiler_params=_params(("parallel",)),
        name="mla_in",
    )(x2d, g0, win, qn, wuq, kvn, wukv, cos_t, sin_t)


def _attn_kernel(q_ref, k_ref, v_ref, g_ref, x_ref, wout_ref, g1_ref, h_ref, hn_ref, o_ref):
    for h in range(HEADS):
        qh = q_ref[0, :, 2 * LANES * h:2 * LANES * (h + 1)]
        kh = k_ref[0, :, 2 * LANES * h:2 * LANES * (h + 1)]
        s = lax.dot_general(qh, kh, (((1,), (1,)), ((), ())), preferred_element_type=F32)
        m = jnp.max(s, axis=-1, keepdims=True)
        p = jnp.exp2((s - m).astype(BF16))
        ov = _dot(p, v_ref[0, :, 2 * LANES * h:2 * LANES * (h + 1)])
        o_ref[:, VDIM * h:VDIM * (h + 1)] = ov[:, :VDIM] / ov[:, VDIM:VDIM + 1]
    og = (o_ref[...] * g_ref[0].astype(F32)).astype(BF16)
    h1 = x_ref[0] + _dot(og, wout_ref[...])
    h_ref[0] = h1
    hn_ref[0] = _rms(h1, g1_ref[...]).astype(BF16)


def _attention(q, k, v, g, x, wout, g1, tq):
    b, seq, d = x.shape
    qspec = lambda w: pl.BlockSpec((1, tq, w), lambda i, j: (i, j, 0))
    kspec = lambda w: pl.BlockSpec((1, seq, w), lambda i, j: (i, 0, 0))
    full = lambda a: pl.BlockSpec(a.shape, lambda i, j: (0,) * a.ndim)
    return pl.pallas_call(
        _attn_kernel,
        grid=(b, seq // tq),
        in_specs=[qspec(q.shape[-1]), kspec(k.shape[-1]), kspec(v.shape[-1]), qspec(g.shape[-1]),
                  qspec(d), full(wout), full(g1)],
        out_specs=[qspec(d), qspec(d)],
        out_shape=[jax.ShapeDtypeStruct((b, seq, d), F32), jax.ShapeDtypeStruct((b, seq, d), BF16)],
        scratch_shapes=[pltpu.VMEM((tq, HEADS * VDIM), F32)],
        compiler_params=_params(("parallel", "arbitrary")),
        name="mla_attn",
    )(q, k, v, g, x, wout, g1)


def _hyena_in_kernel(hn_ref, w_ref, cw_ref, o_ref, *, rc):
    seq, tn = hn_ref.shape[1], w_ref.shape[1]
    halo = 16
    w0, w1, w2, cb = cw_ref[0:1, :], cw_ref[1:2, :], cw_ref[2:3, :], cw_ref[3:4, :]
    rows = lax.broadcasted_iota(jnp.int32, (halo, tn), 0)
    nchunk = seq // rc
    bounds = [(max(rc * r - halo, 0), min(rc * (r + 1) + halo, seq)) for r in range(nchunk)]
    project = lambda r: _dot(hn_ref[0, bounds[r][0]:bounds[r][1], :], w_ref[...])
    nxt_proj = project(0)
    for r in range(nchunk):
        lo, hi = bounds[r]
        n = hi - lo
        proj = nxt_proj
        if r + 1 < nchunk:
            nxt_proj = project(r + 1)
        prev = pltpu.roll(proj, 1, axis=0)
        nxt = pltpu.roll(proj, n - 1, axis=0)
        out = cb + w0 * prev + w1 * proj + w2 * nxt
        first = rc * r - lo
        o_ref[0, rc * r:rc * (r + 1), :] = out[first:first + rc].astype(BF16)
        if r == 0:
            top = cb + w0 * jnp.where(rows == 0, 0.0, prev[:halo]) + w1 * proj[:halo] + w2 * nxt[:halo]
            o_ref[0, :halo, :] = top.astype(BF16)
        if r == nchunk - 1:
            bot = (cb + w0 * prev[n - halo:] + w1 * proj[n - halo:]
                   + w2 * jnp.where(rows == halo - 1, 0.0, nxt[n - halo:]))
            o_ref[0, seq - halo:, :] = bot.astype(BF16)


def _hyena_in(hn, w, cw, tn, rc):
    b, seq, d = hn.shape
    nout = w.shape[1]
    return pl.pallas_call(
        functools.partial(_hyena_in_kernel, rc=rc),
        grid=(b, nout // tn),
        in_specs=[pl.BlockSpec((1, seq, d), lambda i, j: (i, 0, 0)),
                  pl.BlockSpec((d, tn), lambda i, j: (0, j)),
                  pl.BlockSpec((SUBLANES, tn), lambda i, j: (0, j))],
        out_specs=pl.BlockSpec((1, seq, tn), lambda i, j: (i, 0, j)),
        out_shape=jax.ShapeDtypeStruct((b, seq, nout), BF16),
        compiler_params=_params(("parallel", "arbitrary")),
        name="hyena_in",
    )(hn, w, cw)


def _bf16_bits(x):
    bits = lax.bitcast_convert_type(x, jnp.uint32)
    return (bits + jnp.uint32(0x7FFF) + ((bits >> 16) & jnp.uint32(1))) >> 16


def _dot_split(a, b):
    a_hi, b_hi = a.astype(BF16), b.astype(BF16)
    a_lo = (a - a_hi.astype(F32)).astype(BF16)
    b_lo = (b - b_hi.astype(F32)).astype(BF16)
    return _dot(a_hi, b_hi) + _dot(a_hi, b_lo) + _dot(a_lo, b_hi)


def _filter_kernel(z_ref, w1_ref, b1_ref, w2_ref, b2_ref, w3_ref, b3_ref, fr_ref, w4_ref, t_ref, dl_ref,
                   kf_ref, a_ref):
    seq = a_ref.shape[1] // 2
    hi = lax.Precision.HIGHEST

    @pl.when((pl.program_id(0) == 0) & (pl.program_id(1) == 0))
    def _():
        fr = fr_ref[...]
        a = jnp.sin(fr * (jnp.dot(w1_ref[...], z_ref[...], precision=hi, preferred_element_type=F32)
                          + b1_ref[...]))
        a = jnp.sin(fr * (jnp.dot(w2_ref[...], a, precision=hi, preferred_element_type=F32) + b2_ref[...]))
        a = jnp.sin(fr * (jnp.dot(w3_ref[...], a, precision=hi, preferred_element_type=F32) + b3_ref[...]))
        a_ref[...] = a

    bwd = _dot_split(w4_ref[0, 1], a_ref[:, :seq])
    fwd = _dot_split(w4_ref[0, 0], a_ref[:, seq:])
    decay = jnp.exp(-(dl_ref[...] * t_ref[...]))
    kf = jnp.concatenate([bwd, fwd], axis=1) * decay
    kf_ref[0] = _bf16_bits(kf) | (_bf16_bits(pltpu.roll(kf, 1, axis=1)) << 16)


def _filters(zcat, w1t, b1, w2t, b2, w3t, b3, fr, w4t, tcat, deltas, tc):
    _, _, width, hid = w4t.shape
    two_l = zcat.shape[1]
    full = lambda a: pl.BlockSpec(a.shape, lambda n, i: (0,) * a.ndim)
    return pl.pallas_call(
        _filter_kernel,
        grid=(2, width // tc),
        in_specs=[full(zcat), full(w1t), full(b1), full(w2t), full(b2), full(w3t), full(b3), full(fr),
                  pl.BlockSpec((1, 2, tc, hid), lambda n, i: (n, 0, i, 0)),
                  full(tcat), pl.BlockSpec((tc, 1), lambda n, i: (i, 0))],
        out_specs=pl.BlockSpec((1, tc, two_l), lambda n, i: (n, i, 0)),
        out_shape=jax.ShapeDtypeStruct((2, width, two_l), jnp.uint32),
        scratch_shapes=[pltpu.VMEM((hid, two_l), F32)],
        compiler_params=_params(("arbitrary", "arbitrary")),
        name="hyena_filters",
    )(zcat, w1t, b1, w2t, b2, w3t, b3, fr, w4t, tcat, deltas)


def _conv_kernel(x1_ref, x2_ref, v_ref, kf1_ref, kf2_ref, par_ref, o_ref,
                 lhs_ref, rhs1_ref, rhs2_ref, z1_ref, *, ct):
    nj, nb = v_ref.shape[1], v_ref.shape[2]
    ni = nj // 2
    ntile = kf1_ref.shape[1]
    ng = ntile - 2
    pad = 2 * ni - 2
    rows_per_word_tile = 2 * SUBLANES

    @pl.when(pl.program_id(0) == 0)
    def _():
        lhs_ref[...] = jnp.zeros(lhs_ref.shape, lhs_ref.dtype)

    diff = (lax.broadcasted_iota(jnp.int32, (SUBLANES, LANES), 1)
            - 2 * lax.broadcasted_iota(jnp.int32, (SUBLANES, LANES), 0))
    nroll = LANES // rows_per_word_tile

    def store_lhs(s, j, slab):
        for i in range(ni):
            jj = j + pad - 2 * i
            lhs_ref[s, nb * i:nb * (i + 1), LANES * jj:LANES * (jj + 1)] = slab

    def build_rhs(kf_ref, c, rhs_ref):
        prev = None
        for x in range(ntile - 1, -1, -1):
            tile = jnp.broadcast_to(kf_ref[c, x:x + 1, :], (SUBLANES, LANES))
            cur = [pltpu.roll(tile, rows_per_word_tile * e, axis=1, stride=2, stride_axis=0)
                   for e in range(nroll)]
            if prev is not None:
                g = ng - x
                for e in range(nroll):
                    words = jnp.where(diff < rows_per_word_tile * e, cur[e], prev[e])
                    blk = pltpu.bitcast(words, BF16)
                    r = LANES * g + rows_per_word_tile * e
                    if g < ng:
                        rhs_ref[r:r + rows_per_word_tile, LANES:2 * LANES] = blk
                    if g >= 1:
                        rhs_ref[r - LANES:r - LANES + rows_per_word_tile, 0:LANES] = blk
            prev = cur

    def piece(y, j):
        i, half = divmod(j, 2)
        return y[nb * i:nb * (i + 1), LANES * half:LANES * (half + 1)]

    slots = (0, 1)
    for s in slots:
        build_rhs(kf1_ref, s, rhs1_ref.at[s])

    def channel_pair(p, carry):
        cs = [2 * p + s for s in slots]
        nxt = [jnp.minimum(c + 2, ct - 2 + s) for s, c in zip(slots, cs)]
        bias1 = [par_ref[c, 0:1, :] for c in cs]
        bias2 = [par_ref[c, 1:2, :] for c in cs]
        for s, c in zip(slots, cs):
            for j in range(nj):
                store_lhs(s, j, v_ref[c, j])
        y1 = [_dot(lhs_ref[s], rhs1_ref[s]) for s in slots]
        for s, c in zip(slots, cs):
            build_rhs(kf2_ref, c, rhs2_ref.at[s])
        for s, c in zip(slots, cs):
            for j in range(nj):
                z1 = x1_ref[c, j].astype(F32) * (piece(y1[s], j) + bias1[s] * v_ref[c, j].astype(F32))
                z1_ref[s, j] = z1
                store_lhs(s, j, z1.astype(BF16))
        y2 = [_dot(lhs_ref[s], rhs2_ref[s]) for s in slots]
        for s in slots:
            build_rhs(kf1_ref, nxt[s], rhs1_ref.at[s])
        for s, c in zip(slots, cs):
            for j in range(nj):
                z2 = x2_ref[c, j].astype(F32) * (piece(y2[s], j) + bias2[s] * z1_ref[s, j])
                o_ref[c, j] = z2.astype(BF16)
        return carry

    lax.fori_loop(0, ct // 2, channel_pair, 0)


def _long_conv(ut, kf, par, first_channel, ct):
    _, nj, nb, _ = ut.shape
    wp = ut.shape[0] // 3
    seq = nj * LANES
    kdim = 2 * seq - MXU_N
    nblk = wp // ct
    base = first_channel // ct
    ntile = kf.shape[2]
    uspec = lambda o: pl.BlockSpec((ct, nj, nb, LANES), lambda i: (i + o * nblk, 0, 0, 0))
    kspec = lambda n: pl.BlockSpec((None, ct, ntile, LANES), lambda i: (n, i + base, 0, 0))
    return pl.pallas_call(
        functools.partial(_conv_kernel, ct=ct),
        grid=(nblk,),
        in_specs=[uspec(0), uspec(1), uspec(2), kspec(0), kspec(1),
                  pl.BlockSpec((ct, SUBLANES, LANES), lambda i: (i + base, 0, 0))],
        out_specs=pl.BlockSpec((ct, nj, nb, LANES), lambda i: (i, 0, 0, 0)),
        out_shape=jax.ShapeDtypeStruct((wp, nj, nb, LANES), BF16),
        scratch_shapes=[pltpu.VMEM((2, nj // 2 * nb, kdim), BF16),
                        pltpu.VMEM((2, kdim, MXU_N), BF16),
                        pltpu.VMEM((2, kdim, MXU_N), BF16),
                        pltpu.VMEM((2, nj, nb, LANES), F32)],
        compiler_params=_params(("arbitrary",)),
        name="hyena_long_conv",
    )(ut, ut, ut, kf, kf, par)


def _out_kernel(*refs):
    *z_refs, hn_ref, h_ref, wg_ref, w_ref, gf_ref, o_ref = refs
    gate = _dot(hn_ref[...], wg_ref[...])
    z = jnp.concatenate([r[...] for r in z_refs], axis=1).astype(F32)
    zg = (z * (gate * jax.nn.sigmoid(gate))).astype(BF16)
    h2 = h_ref[...] + _dot(zg, w_ref[...])
    o_ref[...] = _rms(h2, gf_ref[...])


def _hyena_out(z_parts, hn2d, h2d, wg, w, gf, tm):
    n, d = h2d.shape
    row = lambda w_: pl.BlockSpec((tm, w_), lambda i: (i, 0))
    full = lambda a: pl.BlockSpec(a.shape, lambda i: (0,) * a.ndim)
    return pl.pallas_call(
        _out_kernel,
        grid=(n // tm,),
        in_specs=[row(z.shape[1]) for z in z_parts] + [row(d), row(d), full(wg), full(w), full(gf)],
        out_specs=row(d),
        out_shape=jax.ShapeDtypeStruct((n, d), F32),
        compiler_params=_params(("parallel",)),
        name="hyena_out",
    )(*z_parts, hn2d, h2d, wg, w, gf)


def _rope_tables(seq):
    inv = 1.0 / (ROPE_THETA ** (jnp.arange(0, ROPE, 2, dtype=F32) / ROPE))
    ang = jnp.arange(seq, dtype=F32)[:, None] * inv[None, :]
    cos, sin = jnp.cos(ang), jnp.sin(ang)
    zero = jnp.zeros((seq, LANES - ROPE), F32)
    return (jnp.concatenate([cos, cos, zero], axis=1), jnp.concatenate([-sin, sin, zero], axis=1))


def _filter_tables(seq, width):
    t = jnp.linspace(0.0, 1.0, seq, dtype=F32)[:, None]
    w = 2.0 * math.pi * jnp.arange(seq, dtype=F32) / seq
    bands = jnp.linspace(1e-4, POS_BANDS - 1, POS_BANDS, dtype=F32)
    fw = w[:, None] * bands[None, :]
    z = jnp.concatenate([t, jnp.cos(fw), -jnp.sin(fw)], axis=-1)
    deltas = jnp.abs(jnp.linspace(MIN_DECAY, MAX_DECAY, width, dtype=F32))
    idx = np.abs(np.arange(2 * seq) - seq)
    idx[0] = 0
    zcat = jnp.pad(z[idx].T, ((0, FILT_HIDDEN - POS_EMB), (0, 0)))
    tcat = t[idx].T.at[:, 0].set(1e4)
    return zcat, tcat, deltas[:, None]


def kernel(x, l0_norm, l0_w_in, l0_q_norm, l0_w_uq, l0_kv_norm, l0_w_ukv, l0_w_out, l1_norm, l1_w_in,
           l1_conv_w, l1_conv_b, l1_filt_w1, l1_filt_b1, l1_filt_w2, l1_filt_b2, l1_filt_w3, l1_filt_b3,
           l1_filt_w4, l1_filt_freq, l1_filt_bias, l1_w_out, final_norm):
    b, seq, d = x.shape
    width = l1_w_out.shape[0]
    n = b * seq
    tm = min(512, seq)
    tq = min(512, seq)
    row = lambda a: a.reshape(1, -1).astype(F32)
    col = lambda a: a.reshape(-1, 1).astype(F32)

    s1, s2, s3 = Q_RANK, Q_RANK + KV_RANK, Q_RANK + KV_RANK + ROPE
    kpe_w = l0_w_in[:, s2:s3]
    win = jnp.concatenate([l0_w_in[:, :s2], kpe_w, kpe_w, l0_w_in[:, s3:]], axis=1).astype(BF16)
    wuq3 = l0_w_uq.reshape(Q_RANK, HEADS, QK_DIM)
    wuq = jnp.concatenate([wuq3, wuq3[:, :, NOPE:]], axis=2).reshape(Q_RANK, -1).astype(BF16)
    wukv3 = l0_w_ukv.reshape(KV_RANK, HEADS, NOPE + VDIM)
    wukv = jnp.concatenate([wukv3[:, :, :NOPE].reshape(KV_RANK, -1),
                            wukv3[:, :, NOPE:].reshape(KV_RANK, -1)], axis=1).astype(BF16)
    cos_t, sin_t = _rope_tables(seq)

    q, k, v, g = _mla_in(x.reshape(n, d), row(l0_norm), win, row(l0_q_norm), wuq, row(l0_kv_norm),
                         wukv, cos_t, sin_t, seq, tm)
    h1, h1n = _attention(q.reshape(b, seq, -1), k.reshape(b, seq, -1), v.reshape(b, seq, -1),
                         g.reshape(b, seq, -1), x, l0_w_out.astype(BF16), row(l1_norm), tq)

    nu = 3 * width
    nj = seq // LANES
    cw = jnp.concatenate([l1_conv_w, l1_conv_b[None, :], jnp.zeros((SUBLANES - 4, nu), F32)],
                         axis=0).astype(F32)
    w_in1 = l1_w_in.astype(BF16)
    parts = 2
    wp = width // parts
    uts = []
    for part in range(parts):
        take = lambda a: jnp.concatenate([a[:, grp * width + part * wp:grp * width + (part + 1) * wp]
                                          for grp in range(3)], axis=1)
        u = _hyena_in(h1n, take(w_in1), take(cw), wp, min(512, seq))
        uts.append(u.reshape(b, nj, LANES, 3 * wp).transpose(3, 1, 0, 2))

    zcat, tcat, deltas = _filter_tables(seq, width)
    w4t = l1_filt_w4.T.reshape(2, 2, width, FILT_HIDDEN).astype(F32)
    w1t = jnp.pad(l1_filt_w1.T.astype(F32), ((0, 0), (0, FILT_HIDDEN - POS_EMB)))
    kf = _filters(zcat, w1t, col(l1_filt_b1), l1_filt_w2.T.astype(F32), col(l1_filt_b2),
                  l1_filt_w3.T.astype(F32), col(l1_filt_b3), col(l1_filt_freq), w4t, tcat, deltas,
                  min(256, width))
    kf = kf.reshape(2, width, 2 * seq // LANES, LANES)

    par = jnp.concatenate([l1_filt_bias.astype(F32), jnp.zeros((SUBLANES - 2, width), F32)], axis=0)
    par = jnp.broadcast_to(par.T[:, :, None], (width, SUBLANES, LANES))

    z_parts = []
    for part in range(parts):
        z2t = _long_conv(uts[part], kf, par, part * wp, 32)
        z_parts.append(z2t.transpose(2, 1, 3, 0).reshape(n, wp))
    out = _hyena_out(z_parts, h1n.reshape(n, d), h1.reshape(n, d), w_in1[:, nu:], l1_w_out.astype(BF16),
                     row(final_norm), tm)
    return out.reshape(b, seq, d)
```

```python
import functools
import math

import jax
import jax.numpy as jnp
import numpy as np
from jax import lax
from jax.experimental import pallas as pl
from jax.experimental.pallas import tpu as pltpu

RMS_EPS = 1e-6
HEADS = 8
NOPE = 128
ROPE = 64
VDIM = 128
Q_RANK = 384
KV_RANK = 256
ROPE_THETA = 10000.0
QK_DIM = NOPE + ROPE

POS_EMB = 33
POS_BANDS = (POS_EMB - 1) // 2
FILT_HIDDEN = 64
MIN_DECAY = math.log(1e-2) / 1.5
MAX_DECAY = math.log(1e-2) / 0.3

LANES = 128
SUBLANES = 8
MXU_N = 256
VMEM_LIMIT = 60 * 1024 * 1024

F32 = jnp.float32
BF16 = jnp.bfloat16


def _rms(x, g):
    return x * lax.rsqrt(jnp.mean(x * x, axis=-1, keepdims=True) + RMS_EPS) * g


def _dot(a, b):
    return jnp.dot(a, b, preferred_element_type=F32)


def _params(sem):
    return pltpu.CompilerParams(dimension_semantics=sem, vmem_limit_bytes=VMEM_LIMIT)


def _mla_in_kernel(x_ref, g0_ref, win_ref, qn_ref, wuq_ref, kvn_ref, wukv_ref,
                   cos_ref, sin_ref, q_ref, k_ref, v_ref, g_ref):
    scale = QK_DIM ** -0.5 * math.log2(math.e)
    xn = _rms(x_ref[...], g0_ref[...]).astype(BF16)
    proj = _dot(xn, win_ref[...])
    c_q = proj[:, :Q_RANK]
    c_kv = proj[:, Q_RANK:Q_RANK + KV_RANK]
    o = Q_RANK + KV_RANK
    kpe = proj[:, o:o + LANES]
    gate = proj[:, o + LANES:]
    cos = cos_ref[...]
    sin = sin_ref[...]

    def rope(pe):
        return pe * cos + pltpu.roll(pe, ROPE // 2, axis=1) * sin

    cqn = _rms(c_q, qn_ref[...]).astype(BF16)
    qf = _dot(cqn, wuq_ref[...])
    for h in range(HEADS):
        nope = qf[:, 2 * LANES * h:2 * LANES * h + LANES]
        pe = rope(qf[:, 2 * LANES * h + LANES:2 * LANES * (h + 1)])
        q_ref[:, 2 * LANES * h:2 * LANES * h + LANES] = (nope * scale).astype(BF16)
        q_ref[:, 2 * LANES * h + LANES:2 * LANES * (h + 1)] = (pe * scale).astype(BF16)

    ckvn = _rms(c_kv, kvn_ref[...]).astype(BF16)
    kv = _dot(ckvn, wukv_ref[...])
    kpe_r = rope(kpe).astype(BF16)
    for h in range(HEADS):
        k_ref[:, 2 * LANES * h:2 * LANES * h + LANES] = kv[:, LANES * h:LANES * (h + 1)].astype(BF16)
        k_ref[:, 2 * LANES * h + LANES:2 * LANES * (h + 1)] = kpe_r
    ones_col = (lax.broadcasted_iota(jnp.int32, (kv.shape[0], LANES), 1) == 0).astype(BF16)
    for h in range(HEADS):
        v_ref[:, 2 * LANES * h:2 * LANES * h + LANES] = kv[:, HEADS * NOPE + VDIM * h:
                                                           HEADS * NOPE + VDIM * (h + 1)].astype(BF16)
        v_ref[:, 2 * LANES * h + LANES:2 * LANES * (h + 1)] = ones_col
    g_ref[...] = (gate * jax.nn.sigmoid(gate)).astype(BF16)


def _mla_in(x2d, g0, win, qn, wuq, kvn, wukv, cos_t, sin_t, seq, tm):
    n, d = x2d.shape
    nt = seq // tm
    full = lambda a: pl.BlockSpec(a.shape, lambda i: (0,) * a.ndim)
    return pl.pallas_call(
        _mla_in_kernel,
        grid=(n // tm,),
        in_specs=[pl.BlockSpec((tm, d), lambda i: (i, 0)), full(g0), full(win), full(qn), full(wuq),
                  full(kvn), full(wukv),
                  pl.BlockSpec((tm, LANES), lambda i: (i % nt, 0)),
                  pl.BlockSpec((tm, LANES), lambda i: (i % nt, 0))],
        out_specs=[pl.BlockSpec((tm, HEADS * 2 * LANES), lambda i: (i, 0)),
                   pl.BlockSpec((tm, HEADS * 2 * LANES), lambda i: (i, 0)),
                   pl.BlockSpec((tm, HEADS * 2 * LANES), lambda i: (i, 0)),
                   pl.BlockSpec((tm, HEADS * VDIM), lambda i: (i, 0))],
        out_shape=[jax.ShapeDtypeStruct((n, HEADS * 2 * LANES), BF16),
                   jax.ShapeDtypeStruct((n, HEADS * 2 * LANES), BF16),
                   jax.ShapeDtypeStruct((n, HEADS * 2 * LANES), BF16),
                   jax.ShapeDtypeStruct((n, HEADS * VDIM), BF16)],
        compiler_params=_params(("parallel",)),
        name="mla_in",
    )(x2d, g0, win, qn, wuq, kvn, wukv, cos_t, sin_t)


def _attn_kernel(q_ref, k_ref, v_ref, g_ref, x_ref, wout_ref, g1_ref, h_ref, hn_ref, o_ref):
    for h in range(HEADS):
        qh = q_ref[0, :, 2 * LANES * h:2 * LANES * (h + 1)]
        kh = k_ref[0, :, 2 * LANES * h:2 * LANES * (h + 1)]
        s = lax.dot_general(qh, kh, (((1,), (1,)), ((), ())), preferred_element_type=F32)
        m = jnp.max(s, axis=-1, keepdims=True)
        p = jnp.exp2((s - m).astype(BF16))
        ov = _dot(p, v_ref[0, :, 2 * LANES * h:2 * LANES * (h + 1)])
        o_ref[:, VDIM * h:VDIM * (h + 1)] = ov[:, :VDIM] / ov[:, VDIM:VDIM + 1]
    og = (o_ref[...] * g_ref[0].astype(F32)).astype(BF16)
    h1 = x_ref[0] + _dot(og, wout_ref[...])
    h_ref[0] = h1
    hn_ref[0] = _rms(h1, g1_ref[...]).astype(BF16)


def _attention(q, k, v, g, x, wout, g1, tq):
    b, seq, d = x.shape
    qspec = lambda w: pl.BlockSpec((1, tq, w), lambda i, j: (i, j, 0))
    kspec = lambda w: pl.BlockSpec((1, seq, w), lambda i, j: (i, 0, 0))
    full = lambda a: pl.BlockSpec(a.shape, lambda i, j: (0,) * a.ndim)
    return pl.pallas_call(
        _attn_kernel,
        grid=(b, seq // tq),
        in_specs=[qspec(q.shape[-1]), kspec(k.shape[-1]), kspec(v.shape[-1]), qspec(g.shape[-1]),
                  qspec(d), full(wout), full(g1)],
        out_specs=[qspec(d), qspec(d)],
        out_shape=[jax.ShapeDtypeStruct((b, seq, d), F32), jax.ShapeDtypeStruct((b, seq, d), BF16)],
        scratch_shapes=[pltpu.VMEM((tq, HEADS * VDIM), F32)],
        compiler_params=_params(("parallel", "arbitrary")),
        name="mla_attn",
    )(q, k, v, g, x, wout, g1)


def _hyena_in_kernel(hn_ref, w_ref, cw_ref, o_ref, *, rc):
    seq, tn = hn_ref.shape[1], w_ref.shape[1]
    halo = 16
    w0, w1, w2, cb = cw_ref[0:1, :], cw_ref[1:2, :], cw_ref[2:3, :], cw_ref[3:4, :]
    rows = lax.broadcasted_iota(jnp.int32, (halo, tn), 0)
    nchunk = seq // rc
    bounds = [(max(rc * r - halo, 0), min(rc * (r + 1) + halo, seq)) for r in range(nchunk)]
    project = lambda r: _dot(hn_ref[0, bounds[r][0]:bounds[r][1], :], w_ref[...])
    nxt_proj = project(0)
    for r in range(nchunk):
        lo, hi = bounds[r]
        n = hi - lo
        proj = nxt_proj
        if r + 1 < nchunk:
            nxt_proj = project(r + 1)
        prev = pltpu.roll(proj, 1, axis=0)
        nxt = pltpu.roll(proj, n - 1, axis=0)
        out = cb + w0 * prev + w1 * proj + w2 * nxt
        first = rc * r - lo
        o_ref[0, rc * r:rc * (r + 1), :] = out[first:first + rc].astype(BF16)
        if r == 0:
            top = cb + w0 * jnp.where(rows == 0, 0.0, prev[:halo]) + w1 * proj[:halo] + w2 * nxt[:halo]
            o_ref[0, :halo, :] = top.astype(BF16)
        if r == nchunk - 1:
            bot = (cb + w0 * prev[n - halo:] + w1 * proj[n - halo:]
                   + w2 * jnp.where(rows == halo - 1, 0.0, nxt[n - halo:]))
            o_ref[0, seq - halo:, :] = bot.astype(BF16)


def _hyena_in(hn, w, cw, tn, rc):
    b, seq, d = hn.shape
    nout = w.shape[1]
    return pl.pallas_call(
        functools.partial(_hyena_in_kernel, rc=rc),
        grid=(b, nout // tn),
        in_specs=[pl.BlockSpec((1, seq, d), lambda i, j: (i, 0, 0)),
                  pl.BlockSpec((d, tn), lambda i, j: (0, j)),
                  pl.BlockSpec((SUBLANES, tn), lambda i, j: (0, j))],
        out_specs=pl.BlockSpec((1, seq, tn), lambda i, j: (i, 0, j)),
        out_shape=jax.ShapeDtypeStruct((b, seq, nout), BF16),
        compiler_params=_params(("parallel", "arbitrary")),
        name="hyena_in",
    )(hn, w, cw)


def _bf16_bits(x):
    bits = lax.bitcast_convert_type(x, jnp.uint32)
    return (bits + jnp.uint32(0x7FFF) + ((bits >> 16) & jnp.uint32(1))) >> 16


def _dot_split(a, b):
    a_hi, b_hi = a.astype(BF16), b.astype(BF16)
    a_lo = (a - a_hi.astype(F32)).astype(BF16)
    b_lo = (b - b_hi.astype(F32)).astype(BF16)
    return _dot(a_hi, b_hi) + _dot(a_hi, b_lo) + _dot(a_lo, b_hi)


def _filter_kernel(z_ref, w1_ref, b1_ref, w2_ref, b2_ref, w3_ref, b3_ref, fr_ref, w4_ref, t_ref, dl_ref,
                   kf_ref, a_ref):
    seq = a_ref.shape[1] // 2
    hi = lax.Precision.HIGHEST

    @pl.when((pl.program_id(0) == 0) & (pl.program_id(1) == 0))
    def _():
        fr = fr_ref[...]
        a = jnp.sin(fr * (jnp.dot(w1_ref[...], z_ref[...], precision=hi, preferred_element_type=F32)
                          + b1_ref[...]))
        a = jnp.sin(fr * (jnp.dot(w2_ref[...], a, precision=hi, preferred_element_type=F32) + b2_ref[...]))
        a = jnp.sin(fr * (jnp.dot(w3_ref[...], a, precision=hi, preferred_element_type=F32) + b3_ref[...]))
        a_ref[...] = a

    bwd = _dot_split(w4_ref[0, 1], a_ref[:, :seq])
    fwd = _dot_split(w4_ref[0, 0], a_ref[:, seq:])
    decay = jnp.exp(-(dl_ref[...] * t_ref[...]))
    kf = jnp.concatenate([bwd, fwd], axis=1) * decay
    kf_ref[0] = _bf16_bits(kf) | (_bf16_bits(pltpu.roll(kf, 1, axis=1)) << 16)


def _filters(zcat, w1t, b1, w2t, b2, w3t, b3, fr, w4t, tcat, deltas, tc):
    _, _, width, hid = w4t.shape
    two_l = zcat.shape[1]
    full = lambda a: pl.BlockSpec(a.shape, lambda n, i: (0,) * a.ndim)
    return pl.pallas_call(
        _filter_kernel,
        grid=(2, width // tc),
        in_specs=[full(zcat), full(w1t), full(b1), full(w2t), full(b2), full(w3t), full(b3), full(fr),
                  pl.BlockSpec((1, 2, tc, hid), lambda n, i: (n, 0, i, 0)),
                  full(tcat), pl.BlockSpec((tc, 1), lambda n, i: (i, 0))],
        out_specs=pl.BlockSpec((1, tc, two_l), lambda n, i: (n, i, 0)),
        out_shape=jax.ShapeDtypeStruct((2, width, two_l), jnp.uint32),
        scratch_shapes=[pltpu.VMEM((hid, two_l), F32)],
        compiler_params=_params(("arbitrary", "arbitrary")),
        name="hyena_filters",
    )(zcat, w1t, b1, w2t, b2, w3t, b3, fr, w4t, tcat, deltas)


def _conv_kernel(x1_ref, x2_ref, v_ref, kf1_ref, kf2_ref, par_ref, o_ref,
                 lhs_ref, rhs1_ref, rhs2_ref, z1_ref, *, ct):
    nj, nb = v_ref.shape[1], v_ref.shape[2]
    ni = nj // 2
    ntile = kf1_ref.shape[1]
    ng = ntile - 2
    pad = 2 * ni - 2
    rows_per_word_tile = 2 * SUBLANES

    @pl.when(pl.program_id(0) == 0)
    def _():
        lhs_ref[...] = jnp.zeros(lhs_ref.shape, lhs_ref.dtype)

    diff = (lax.broadcasted_iota(jnp.int32, (SUBLANES, LANES), 1)
            - 2 * lax.broadcasted_iota(jnp.int32, (SUBLANES, LANES), 0))
    nroll = LANES // rows_per_word_tile

    def store_lhs(s, j, slab):
        for i in range(ni):
            jj = j + pad - 2 * i
            lhs_ref[s, nb * i:nb * (i + 1), LANES * jj:LANES * (jj + 1)] = slab

    def build_rhs(kf_ref, c, rhs_ref):
        prev = None
        for x in range(ntile - 1, -1, -1):
            tile = jnp.broadcast_to(kf_ref[c, x:x + 1, :], (SUBLANES, LANES))
            cur = [pltpu.roll(tile, rows_per_word_tile * e, axis=1, stride=2, stride_axis=0)
                   for e in range(nroll)]
            if prev is not None:
                g = ng - x
                for e in range(nroll):
                    words = jnp.where(diff < rows_per_word_tile * e, cur[e], prev[e])
                    blk = pltpu.bitcast(words, BF16)
                    r = LANES * g + rows_per_word_tile * e
                    if g < ng:
                        rhs_ref[r:r + rows_per_word_tile, LANES:2 * LANES] = blk
                    if g >= 1:
                        rhs_ref[r - LANES:r - LANES + rows_per_word_tile, 0:LANES] = blk
            prev = cur

    def piece(y, j):
        i, half = divmod(j, 2)
        return y[nb * i:nb * (i + 1), LANES * half:LANES * (half + 1)]

    slots = (0, 1)
    for s in slots:
        build_rhs(kf1_ref, s, rhs1_ref.at[s])

    def channel_pair(p, carry):
        cs = [2 * p + s for s in slots]
        nxt = [jnp.minimum(c + 2, ct - 2 + s) for s, c in zip(slots, cs)]
        bias1 = [par_ref[c, 0:1, :] for c in cs]
        bias2 = [par_ref[c, 1:2, :] for c in cs]
        for s, c in zip(slots, cs):
            for j in range(nj):
                store_lhs(s, j, v_ref[c, j])
        y1 = [_dot(lhs_ref[s], rhs1_ref[s]) for s in slots]
        for s, c in zip(slots, cs):
            build_rhs(kf2_ref, c, rhs2_ref.at[s])
        for s, c in zip(slots, cs):
            for j in range(nj):
                z1 = x1_ref[c, j].astype(F32) * (piece(y1[s], j) + bias1[s] * v_ref[c, j].astype(F32))
                z1_ref[s, j] = z1
                store_lhs(s, j, z1.astype(BF16))
        y2 = [_dot(lhs_ref[s], rhs2_ref[s]) for s in slots]
        for s in slots:
            build_rhs(kf1_ref, nxt[s], rhs1_ref.at[s])
        for s, c in zip(slots, cs):
            for j in range(nj):
                z2 = x2_ref[c, j].astype(F32) * (piece(y2[s], j) + bias2[s] * z1_ref[s, j])
                o_ref[c, j] = z2.astype(BF16)
        return carry

    lax.fori_loop(0, ct // 2, channel_pair, 0)


def _long_conv(ut, kf, par, first_channel, ct):
    _, nj, nb, _ = ut.shape
    wp = ut.shape[0] // 3
    seq = nj * LANES
    kdim = 2 * seq - MXU_N
    nblk = wp // ct
    base = first_channel // ct
    ntile = kf.shape[2]
    uspec = lambda o: pl.BlockSpec((ct, nj, nb, LANES), lambda i: (i + o * nblk, 0, 0, 0))
    kspec = lambda n: pl.BlockSpec((None, ct, ntile, LANES), lambda i: (n, i + base, 0, 0))
    return pl.pallas_call(
        functools.partial(_conv_kernel, ct=ct),
        grid=(nblk,),
        in_specs=[uspec(0), uspec(1), uspec(2), kspec(0), kspec(1),
                  pl.BlockSpec((ct, SUBLANES, LANES), lambda i: (i + base, 0, 0))],
        out_specs=pl.BlockSpec((ct, nj, nb, LANES), lambda i: (i, 0, 0, 0)),
        out_shape=jax.ShapeDtypeStruct((wp, nj, nb, LANES), BF16),
        scratch_shapes=[pltpu.VMEM((2, nj // 2 * nb, kdim), BF16),
                        pltpu.VMEM((2, kdim, MXU_N), BF16),
                        pltpu.VMEM((2, kdim, MXU_N), BF16),
                        pltpu.VMEM((2, nj, nb, LANES), F32)],
        compiler_params=_params(("arbitrary",)),
        name="hyena_long_conv",
    )(ut, ut, ut, kf, kf, par)


def _out_kernel(*refs):
    *z_refs, hn_ref, h_ref, wg_ref, w_ref, gf_ref, o_ref = refs
    gate = _dot(hn_ref[...], wg_ref[...])
    z = jnp.concatenate([r[...] for r in z_refs], axis=1).astype(F32)
    zg = (z * (gate * jax.nn.sigmoid(gate))).astype(BF16)
    h2 = h_ref[...] + _dot(zg, w_ref[...])
    o_ref[...] = _rms(h2, gf_ref[...])


def _hyena_out(z_parts, hn2d, h2d, wg, w, gf, tm):
    n, d = h2d.shape
    row = lambda w_: pl.BlockSpec((tm, w_), lambda i: (i, 0))
    full = lambda a: pl.BlockSpec(a.shape, lambda i: (0,) * a.ndim)
    return pl.pallas_call(
        _out_kernel,
        grid=(n // tm,),
        in_specs=[row(z.shape[1]) for z in z_parts] + [row(d), row(d), full(wg), full(w), full(gf)],
        out_specs=row(d),
        out_shape=jax.ShapeDtypeStruct((n, d), F32),
        compiler_params=_params(("parallel",)),
        name="hyena_out",
    )(*z_parts, hn2d, h2d, wg, w, gf)


def _rope_tables(seq):
    inv = 1.0 / (ROPE_THETA ** (jnp.arange(0, ROPE, 2, dtype=F32) / ROPE))
    ang = jnp.arange(seq, dtype=F32)[:, None] * inv[None, :]
    cos, sin = jnp.cos(ang), jnp.sin(ang)
    zero = jnp.zeros((seq, LANES - ROPE), F32)
    return (jnp.concatenate([cos, cos, zero], axis=1), jnp.concatenate([-sin, sin, zero], axis=1))


def _filter_tables(seq, width):
    t = np.linspace(0.0, 1.0, seq)[:, None]
    w = 2.0 * math.pi * np.arange(seq) / seq
    bands = np.linspace(1e-4, POS_BANDS - 1, POS_BANDS)
    fw = w[:, None] * bands[None, :]
    z = np.concatenate([t, np.cos(fw), -np.sin(fw)], axis=-1)
    deltas = np.abs(np.linspace(MIN_DECAY, MAX_DECAY, width))
    idx = np.abs(np.arange(2 * seq) - seq)
    idx[0] = 0
    zcat = np.pad(z[idx].T, ((0, FILT_HIDDEN - POS_EMB), (0, 0)))
    tcat = t[idx].T.copy()
    tcat[:, 0] = 1e4
    return zcat.astype(np.float32), tcat.astype(np.float32), deltas[:, None].astype(np.float32)


def kernel(x, l0_norm, l0_w_in, l0_q_norm, l0_w_uq, l0_kv_norm, l0_w_ukv, l0_w_out, l1_norm, l1_w_in,
           l1_conv_w, l1_conv_b, l1_filt_w1, l1_filt_b1, l1_filt_w2, l1_filt_b2, l1_filt_w3, l1_filt_b3,
           l1_filt_w4, l1_filt_freq, l1_filt_bias, l1_w_out, final_norm):
    b, seq, d = x.shape
    width = l1_w_out.shape[0]
    n = b * seq
    tm = min(512, seq)
    tq = min(512, seq)
    row = lambda a: a.reshape(1, -1).astype(F32)
    col = lambda a: a.reshape(-1, 1).astype(F32)

    s1, s2, s3 = Q_RANK, Q_RANK + KV_RANK, Q_RANK + KV_RANK + ROPE
    kpe_w = l0_w_in[:, s2:s3]
    win = jnp.concatenate([l0_w_in[:, :s2], kpe_w, kpe_w, l0_w_in[:, s3:]], axis=1).astype(BF16)
    wuq3 = l0_w_uq.reshape(Q_RANK, HEADS, QK_DIM)
    wuq = jnp.concatenate([wuq3, wuq3[:, :, NOPE:]], axis=2).reshape(Q_RANK, -1).astype(BF16)
    wukv3 = l0_w_ukv.reshape(KV_RANK, HEADS, NOPE + VDIM)
    wukv = jnp.concatenate([wukv3[:, :, :NOPE].reshape(KV_RANK, -1),
                            wukv3[:, :, NOPE:].reshape(KV_RANK, -1)], axis=1).astype(BF16)
    cos_t, sin_t = _rope_tables(seq)

    q, k, v, g = _mla_in(x.reshape(n, d), row(l0_norm), win, row(l0_q_norm), wuq, row(l0_kv_norm),
                         wukv, cos_t, sin_t, seq, tm)
    h1, h1n = _attention(q.reshape(b, seq, -1), k.reshape(b, seq, -1), v.reshape(b, seq, -1),
                         g.reshape(b, seq, -1), x, l0_w_out.astype(BF16), row(l1_norm), tq)

    nu = 3 * width
    nj = seq // LANES
    cw = jnp.concatenate([l1_conv_w, l1_conv_b[None, :], jnp.zeros((SUBLANES - 4, nu), F32)],
                         axis=0).astype(F32)
    w_in1 = l1_w_in.astype(BF16)
    parts = 2
    wp = width // parts
    uts = []
    for part in range(parts):
        take = lambda a: jnp.concatenate([a[:, grp * width + part * wp:grp * width + (part + 1) * wp]
                                          for grp in range(3)], axis=1)
        u = _hyena_in(h1n, take(w_in1), take(cw), 3 * wp, min(512, seq))
        uts.append(u.reshape(b, nj, LANES, 3 * wp).transpose(3, 1, 0, 2))

    zcat, tcat, deltas = _filter_tables(seq, width)
    w4t = l1_filt_w4.T.reshape(2, 2, width, FILT_HIDDEN).astype(F32)
    w1t = jnp.pad(l1_filt_w1.T.astype(F32), ((0, 0), (0, FILT_HIDDEN - POS_EMB)))
    kf = _filters(zcat, w1t, col(l1_filt_b1), l1_filt_w2.T.astype(F32), col(l1_filt_b2),
                  l1_filt_w3.T.astype(F32), col(l1_filt_b3), col(l1_filt_freq), w4t, tcat, deltas,
                  min(256, width))
    kf = kf.reshape(2, width, 2 * seq // LANES, LANES)

    par = jnp.concatenate([l1_filt_bias.astype(F32), jnp.zeros((SUBLANES - 2, width), F32)], axis=0)
    par = jnp.broadcast_to(par.T[:, :, None], (width, SUBLANES, LANES))

    z_parts = []
    for part in range(parts):
        z2t = _long_conv(uts[part], kf, par, part * wp, 32)
        z_parts.append(z2t.transpose(2, 1, 3, 0).reshape(n, wp))
    out = _hyena_out(z_parts, h1n.reshape(n, d), h1.reshape(n, d), w_in1[:, nu:], l1_w_out.astype(BF16),
                     row(final_norm), tm)
    return out.reshape(b, seq, d)
```

```python
import functools
import math

import jax
import jax.numpy as jnp
import numpy as np
from jax import lax
from jax.experimental import pallas as pl
from jax.experimental.pallas import tpu as pltpu

RMS_EPS = 1e-6
HEADS = 8
NOPE = 128
ROPE = 64
VDIM = 128
Q_RANK = 384
KV_RANK = 256
ROPE_THETA = 10000.0
QK_DIM = NOPE + ROPE

POS_EMB = 33
POS_BANDS = (POS_EMB - 1) // 2
FILT_HIDDEN = 64
MIN_DECAY = math.log(1e-2) / 1.5
MAX_DECAY = math.log(1e-2) / 0.3

LANES = 128
SUBLANES = 8
MXU_N = 256
VMEM_LIMIT = 60 * 1024 * 1024

F32 = jnp.float32
BF16 = jnp.bfloat16


def _rms(x, g):
    return x * lax.rsqrt(jnp.mean(x * x, axis=-1, keepdims=True) + RMS_EPS) * g


def _dot(a, b):
    return jnp.dot(a, b, preferred_element_type=F32)


def _params(sem):
    return pltpu.CompilerParams(dimension_semantics=sem, vmem_limit_bytes=VMEM_LIMIT)


def _mla_in_kernel(x_ref, g0_ref, win_ref, qn_ref, wuq_ref, kvn_ref, wukv_ref,
                   cos_ref, sin_ref, q_ref, k_ref, v_ref, g_ref):
    scale = QK_DIM ** -0.5 * math.log2(math.e)
    xn = _rms(x_ref[...], g0_ref[...]).astype(BF16)
    proj = _dot(xn, win_ref[...])
    c_q = proj[:, :Q_RANK]
    c_kv = proj[:, Q_RANK:Q_RANK + KV_RANK]
    o = Q_RANK + KV_RANK
    kpe = proj[:, o:o + LANES]
    gate = proj[:, o + LANES:]
    cos = cos_ref[...]
    sin = sin_ref[...]

    def rope(pe):
        return pe * cos + pltpu.roll(pe, ROPE // 2, axis=1) * sin

    cqn = _rms(c_q, qn_ref[...]).astype(BF16)
    qf = _dot(cqn, wuq_ref[...])
    for h in range(HEADS):
        nope = qf[:, 2 * LANES * h:2 * LANES * h + LANES]
        pe = rope(qf[:, 2 * LANES * h + LANES:2 * LANES * (h + 1)])
        q_ref[:, 2 * LANES * h:2 * LANES * h + LANES] = (nope * scale).astype(BF16)
        q_ref[:, 2 * LANES * h + LANES:2 * LANES * (h + 1)] = (pe * scale).astype(BF16)

    ckvn = _rms(c_kv, kvn_ref[...]).astype(BF16)
    kv = _dot(ckvn, wukv_ref[...])
    kpe_r = rope(kpe).astype(BF16)
    for h in range(HEADS):
        k_ref[:, 2 * LANES * h:2 * LANES * h + LANES] = kv[:, LANES * h:LANES * (h + 1)].astype(BF16)
        k_ref[:, 2 * LANES * h + LANES:2 * LANES * (h + 1)] = kpe_r
    ones_col = (lax.broadcasted_iota(jnp.int32, (kv.shape[0], LANES), 1) == 0).astype(BF16)
    for h in range(HEADS):
        v_ref[:, 2 * LANES * h:2 * LANES * h + LANES] = kv[:, HEADS * NOPE + VDIM * h:
                                                           HEADS * NOPE + VDIM * (h + 1)].astype(BF16)
        v_ref[:, 2 * LANES * h + LANES:2 * LANES * (h + 1)] = ones_col
    g_ref[...] = (gate * jax.nn.sigmoid(gate)).astype(BF16)


def _mla_in(x2d, g0, win, qn, wuq, kvn, wukv, cos_t, sin_t, seq, tm):
    n, d = x2d.shape
    nt = seq // tm
    full = lambda a: pl.BlockSpec(a.shape, lambda i: (0,) * a.ndim)
    return pl.pallas_call(
        _mla_in_kernel,
        grid=(n // tm,),
        in_specs=[pl.BlockSpec((tm, d), lambda i: (i, 0)), full(g0), full(win), full(qn), full(wuq),
                  full(kvn), full(wukv),
                  pl.BlockSpec((tm, LANES), lambda i: (i % nt, 0)),
                  pl.BlockSpec((tm, LANES), lambda i: (i % nt, 0))],
        out_specs=[pl.BlockSpec((tm, HEADS * 2 * LANES), lambda i: (i, 0)),
                   pl.BlockSpec((tm, HEADS * 2 * LANES), lambda i: (i, 0)),
                   pl.BlockSpec((tm, HEADS * 2 * LANES), lambda i: (i, 0)),
                   pl.BlockSpec((tm, HEADS * VDIM), lambda i: (i, 0))],
        out_shape=[jax.ShapeDtypeStruct((n, HEADS * 2 * LANES), BF16),
                   jax.ShapeDtypeStruct((n, HEADS * 2 * LANES), BF16),
                   jax.ShapeDtypeStruct((n, HEADS * 2 * LANES), BF16),
                   jax.ShapeDtypeStruct((n, HEADS * VDIM), BF16)],
        compiler_params=_params(("parallel",)),
        name="mla_in",
    )(x2d, g0, win, qn, wuq, kvn, wukv, cos_t, sin_t)


def _attn_kernel(q_ref, k_ref, v_ref, g_ref, x_ref, wout_ref, g1_ref, h_ref, hn_ref, o_ref):
    for h in range(HEADS):
        qh = q_ref[0, :, 2 * LANES * h:2 * LANES * (h + 1)]
        kh = k_ref[0, :, 2 * LANES * h:2 * LANES * (h + 1)]
        s = lax.dot_general(qh, kh, (((1,), (1,)), ((), ())), preferred_element_type=F32)
        m = jnp.max(s, axis=-1, keepdims=True)
        p = jnp.exp2((s - m).astype(BF16))
        ov = _dot(p, v_ref[0, :, 2 * LANES * h:2 * LANES * (h + 1)])
        o_ref[:, VDIM * h:VDIM * (h + 1)] = ov[:, :VDIM] / ov[:, VDIM:VDIM + 1]
    og = (o_ref[...] * g_ref[0].astype(F32)).astype(BF16)
    h1 = x_ref[0] + _dot(og, wout_ref[...])
    h_ref[0] = h1
    hn_ref[0] = _rms(h1, g1_ref[...]).astype(BF16)


def _attention(q, k, v, g, x, wout, g1, tq):
    b, seq, d = x.shape
    qspec = lambda w: pl.BlockSpec((1, tq, w), lambda i, j: (i, j, 0))
    kspec = lambda w: pl.BlockSpec((1, seq, w), lambda i, j: (i, 0, 0))
    full = lambda a: pl.BlockSpec(a.shape, lambda i, j: (0,) * a.ndim)
    return pl.pallas_call(
        _attn_kernel,
        grid=(b, seq // tq),
        in_specs=[qspec(q.shape[-1]), kspec(k.shape[-1]), kspec(v.shape[-1]), qspec(g.shape[-1]),
                  qspec(d), full(wout), full(g1)],
        out_specs=[qspec(d), qspec(d)],
        out_shape=[jax.ShapeDtypeStruct((b, seq, d), F32), jax.ShapeDtypeStruct((b, seq, d), BF16)],
        scratch_shapes=[pltpu.VMEM((tq, HEADS * VDIM), F32)],
        compiler_params=_params(("parallel", "arbitrary")),
        name="mla_attn",
    )(q, k, v, g, x, wout, g1)


def _hyena_in_kernel(hn_ref, w_ref, cw_ref, o_ref, *, rc):
    seq, tn = hn_ref.shape[1], w_ref.shape[1]
    halo = 16
    w0, w1, w2, cb = cw_ref[0:1, :], cw_ref[1:2, :], cw_ref[2:3, :], cw_ref[3:4, :]
    rows = lax.broadcasted_iota(jnp.int32, (halo, tn), 0)
    nchunk = seq // rc
    bounds = [(max(rc * r - halo, 0), min(rc * (r + 1) + halo, seq)) for r in range(nchunk)]
    project = lambda r: _dot(hn_ref[0, bounds[r][0]:bounds[r][1], :], w_ref[...])
    nxt_proj = project(0)
    for r in range(nchunk):
        lo, hi = bounds[r]
        n = hi - lo
        proj = nxt_proj
        if r + 1 < nchunk:
            nxt_proj = project(r + 1)
        prev = pltpu.roll(proj, 1, axis=0)
        nxt = pltpu.roll(proj, n - 1, axis=0)
        out = cb + w0 * prev + w1 * proj + w2 * nxt
        first = rc * r - lo
        o_ref[0, rc * r:rc * (r + 1), :] = out[first:first + rc].astype(BF16)
        if r == 0:
            top = cb + w0 * jnp.where(rows == 0, 0.0, prev[:halo]) + w1 * proj[:halo] + w2 * nxt[:halo]
            o_ref[0, :halo, :] = top.astype(BF16)
        if r == nchunk - 1:
            bot = (cb + w0 * prev[n - halo:] + w1 * proj[n - halo:]
                   + w2 * jnp.where(rows == halo - 1, 0.0, nxt[n - halo:]))
            o_ref[0, seq - halo:, :] = bot.astype(BF16)


def _hyena_in(hn, w, cw, tn, rc):
    b, seq, d = hn.shape
    nout = w.shape[1]
    return pl.pallas_call(
        functools.partial(_hyena_in_kernel, rc=rc),
        grid=(b, nout // tn),
        in_specs=[pl.BlockSpec((1, seq, d), lambda i, j: (i, 0, 0)),
                  pl.BlockSpec((d, tn), lambda i, j: (0, j)),
                  pl.BlockSpec((SUBLANES, tn), lambda i, j: (0, j))],
        out_specs=pl.BlockSpec((1, seq, tn), lambda i, j: (i, 0, j)),
        out_shape=jax.ShapeDtypeStruct((b, seq, nout), BF16),
        compiler_params=_params(("parallel", "arbitrary")),
        name="hyena_in",
    )(hn, w, cw)


def _bf16_bits(x):
    bits = lax.bitcast_convert_type(x, jnp.uint32)
    return (bits + jnp.uint32(0x7FFF) + ((bits >> 16) & jnp.uint32(1))) >> 16


def _dot_split(a, b):
    a_hi, b_hi = a.astype(BF16), b.astype(BF16)
    a_lo = (a - a_hi.astype(F32)).astype(BF16)
    b_lo = (b - b_hi.astype(F32)).astype(BF16)
    return _dot(a_hi, b_hi) + _dot(a_hi, b_lo) + _dot(a_lo, b_hi)


def _filter_kernel(z_ref, w1_ref, b1_ref, w2_ref, b2_ref, w3_ref, b3_ref, fr_ref, w4_ref, t_ref, dl_ref,
                   kf_ref, a_ref):
    seq = a_ref.shape[1] // 2
    hi = lax.Precision.HIGHEST

    @pl.when((pl.program_id(0) == 0) & (pl.program_id(1) == 0))
    def _():
        fr = fr_ref[...]
        a = jnp.sin(fr * (jnp.dot(w1_ref[...], z_ref[...], precision=hi, preferred_element_type=F32)
                          + b1_ref[...]))
        a = jnp.sin(fr * (jnp.dot(w2_ref[...], a, precision=hi, preferred_element_type=F32) + b2_ref[...]))
        a = jnp.sin(fr * (jnp.dot(w3_ref[...], a, precision=hi, preferred_element_type=F32) + b3_ref[...]))
        a_ref[...] = a

    bwd = _dot_split(w4_ref[0, 1], a_ref[:, :seq])
    fwd = _dot_split(w4_ref[0, 0], a_ref[:, seq:])
    decay = jnp.exp(-(dl_ref[...] * t_ref[...]))
    kf = jnp.concatenate([bwd, fwd], axis=1) * decay
    bits = _bf16_bits(kf)
    kf_ref[0] = bits | (pltpu.roll(bits, 1, axis=1) << 16)


def _filters(zcat, w1t, b1, w2t, b2, w3t, b3, fr, w4t, tcat, deltas, tc):
    _, _, width, hid = w4t.shape
    two_l = zcat.shape[1]
    full = lambda a: pl.BlockSpec(a.shape, lambda n, i: (0,) * a.ndim)
    return pl.pallas_call(
        _filter_kernel,
        grid=(2, width // tc),
        in_specs=[full(zcat), full(w1t), full(b1), full(w2t), full(b2), full(w3t), full(b3), full(fr),
                  pl.BlockSpec((1, 2, tc, hid), lambda n, i: (n, 0, i, 0)),
                  full(tcat), pl.BlockSpec((tc, 1), lambda n, i: (i, 0))],
        out_specs=pl.BlockSpec((1, tc, two_l), lambda n, i: (n, i, 0)),
        out_shape=jax.ShapeDtypeStruct((2, width, two_l), jnp.uint32),
        scratch_shapes=[pltpu.VMEM((hid, two_l), F32)],
        compiler_params=_params(("arbitrary", "arbitrary")),
        name="hyena_filters",
    )(zcat, w1t, b1, w2t, b2, w3t, b3, fr, w4t, tcat, deltas)


def _conv_kernel(x1_ref, x2_ref, v_ref, kf1_ref, kf2_ref, par_ref, o_ref,
                 lhs_ref, rhs1_ref, rhs2_ref, z1_ref, *, ct):
    nj, nb = v_ref.shape[1], v_ref.shape[2]
    ni = nj // 2
    ntile = kf1_ref.shape[1]
    ng = ntile - 2
    pad = 2 * ni - 2
    rows_per_word_tile = 2 * SUBLANES

    @pl.when(pl.program_id(0) == 0)
    def _():
        lhs_ref[...] = jnp.zeros(lhs_ref.shape, lhs_ref.dtype)

    diff = (lax.broadcasted_iota(jnp.int32, (SUBLANES, LANES), 1)
            - 2 * lax.broadcasted_iota(jnp.int32, (SUBLANES, LANES), 0))
    nroll = LANES // rows_per_word_tile

    def store_lhs(s, j, slab):
        for i in range(ni):
            jj = j + pad - 2 * i
            lhs_ref[s, nb * i:nb * (i + 1), LANES * jj:LANES * (jj + 1)] = slab

    def build_rhs(kf_ref, c, rhs_ref):
        prev = None
        for x in range(ntile - 1, -1, -1):
            tile = jnp.broadcast_to(kf_ref[c, x:x + 1, :], (SUBLANES, LANES))
            cur = [pltpu.roll(tile, rows_per_word_tile * e, axis=1, stride=2, stride_axis=0)
                   for e in range(nroll)]
            if prev is not None:
                g = ng - x
                for e in range(nroll):
                    words = jnp.where(diff < rows_per_word_tile * e, cur[e], prev[e])
                    blk = pltpu.bitcast(words, BF16)
                    r = LANES * g + rows_per_word_tile * e
                    if g < ng:
                        rhs_ref[r:r + rows_per_word_tile, LANES:2 * LANES] = blk
                    if g >= 1:
                        rhs_ref[r - LANES:r - LANES + rows_per_word_tile, 0:LANES] = blk
            prev = cur

    def piece(y, j):
        i, half = divmod(j, 2)
        return y[nb * i:nb * (i + 1), LANES * half:LANES * (half + 1)]

    slots = (0, 1)
    for s in slots:
        build_rhs(kf1_ref, s, rhs1_ref.at[s])

    def channel_pair(p, carry):
        cs = [2 * p + s for s in slots]
        nxt = [jnp.minimum(c + 2, ct - 2 + s) for s, c in zip(slots, cs)]
        bias1 = [par_ref[c, 0:1, :] for c in cs]
        bias2 = [par_ref[c, 1:2, :] for c in cs]
        for s, c in zip(slots, cs):
            for j in range(nj):
                store_lhs(s, j, v_ref[c, j])
        y1 = [_dot(lhs_ref[s], rhs1_ref[s]) for s in slots]
        for s, c in zip(slots, cs):
            build_rhs(kf2_ref, c, rhs2_ref.at[s])
        for s, c in zip(slots, cs):
            for j in range(nj):
                z1 = x1_ref[c, j].astype(F32) * (piece(y1[s], j) + bias1[s] * v_ref[c, j].astype(F32))
                z1_ref[s, j] = z1
                store_lhs(s, j, z1.astype(BF16))
        y2 = [_dot(lhs_ref[s], rhs2_ref[s]) for s in slots]
        for s in slots:
            build_rhs(kf1_ref, nxt[s], rhs1_ref.at[s])
        for s, c in zip(slots, cs):
            for j in range(nj):
                z2 = x2_ref[c, j].astype(F32) * (piece(y2[s], j) + bias2[s] * z1_ref[s, j])
                o_ref[c, j] = z2.astype(BF16)
        return carry

    lax.fori_loop(0, ct // 2, channel_pair, 0)


def _long_conv(ut, kf, par, first_channel, ct):
    _, nj, nb, _ = ut.shape
    wp = ut.shape[0] // 3
    seq = nj * LANES
    kdim = 2 * seq - MXU_N
    nblk = wp // ct
    base = first_channel // ct
    ntile = kf.shape[2]
    uspec = lambda o: pl.BlockSpec((ct, nj, nb, LANES), lambda i: (i + o * nblk, 0, 0, 0))
    kspec = lambda n: pl.BlockSpec((None, ct, ntile, LANES), lambda i: (n, i + base, 0, 0))
    return pl.pallas_call(
        functools.partial(_conv_kernel, ct=ct),
        grid=(nblk,),
        in_specs=[uspec(0), uspec(1), uspec(2), kspec(0), kspec(1),
                  pl.BlockSpec((ct, SUBLANES, LANES), lambda i: (i + base, 0, 0))],
        out_specs=pl.BlockSpec((ct, nj, nb, LANES), lambda i: (i, 0, 0, 0)),
        out_shape=jax.ShapeDtypeStruct((wp, nj, nb, LANES), BF16),
        scratch_shapes=[pltpu.VMEM((2, nj // 2 * nb, kdim), BF16),
                        pltpu.VMEM((2, kdim, MXU_N), BF16),
                        pltpu.VMEM((2, kdim, MXU_N), BF16),
                        pltpu.VMEM((2, nj, nb, LANES), F32)],
        compiler_params=_params(("arbitrary",)),
        name="hyena_long_conv",
    )(ut, ut, ut, kf, kf, par)


def _out_kernel(*refs):
    *z_refs, hn_ref, h_ref, wg_ref, w_ref, gf_ref, o_ref = refs
    gate = _dot(hn_ref[...], wg_ref[...])
    z = jnp.concatenate([r[...] for r in z_refs], axis=1).astype(F32)
    zg = (z * (gate * jax.nn.sigmoid(gate))).astype(BF16)
    h2 = h_ref[...] + _dot(zg, w_ref[...])
    o_ref[...] = _rms(h2, gf_ref[...])


def _hyena_out(z_parts, hn2d, h2d, wg, w, gf, tm):
    n, d = h2d.shape
    row = lambda w_: pl.BlockSpec((tm, w_), lambda i: (i, 0))
    full = lambda a: pl.BlockSpec(a.shape, lambda i: (0,) * a.ndim)
    return pl.pallas_call(
        _out_kernel,
        grid=(n // tm,),
        in_specs=[row(z.shape[1]) for z in z_parts] + [row(d), row(d), full(wg), full(w), full(gf)],
        out_specs=row(d),
        out_shape=jax.ShapeDtypeStruct((n, d), F32),
        compiler_params=_params(("parallel",)),
        name="hyena_out",
    )(*z_parts, hn2d, h2d, wg, w, gf)


def _rope_tables(seq):
    inv = 1.0 / (ROPE_THETA ** (jnp.arange(0, ROPE, 2, dtype=F32) / ROPE))
    ang = jnp.arange(seq, dtype=F32)[:, None] * inv[None, :]
    cos, sin = jnp.cos(ang), jnp.sin(ang)
    zero = jnp.zeros((seq, LANES - ROPE), F32)
    return (jnp.concatenate([cos, cos, zero], axis=1), jnp.concatenate([-sin, sin, zero], axis=1))


def _filter_tables(seq, width):
    t = np.linspace(0.0, 1.0, seq)[:, None]
    w = 2.0 * math.pi * np.arange(seq) / seq
    bands = np.linspace(1e-4, POS_BANDS - 1, POS_BANDS)
    fw = w[:, None] * bands[None, :]
    z = np.concatenate([t, np.cos(fw), -np.sin(fw)], axis=-1)
    deltas = np.abs(np.linspace(MIN_DECAY, MAX_DECAY, width))
    idx = np.abs(np.arange(2 * seq) - seq)
    idx[0] = 0
    zcat = np.pad(z[idx].T, ((0, FILT_HIDDEN - POS_EMB), (0, 0)))
    tcat = t[idx].T.copy()
    tcat[:, 0] = 1e4
    return zcat.astype(np.float32), tcat.astype(np.float32), deltas[:, None].astype(np.float32)


def kernel(x, l0_norm, l0_w_in, l0_q_norm, l0_w_uq, l0_kv_norm, l0_w_ukv, l0_w_out, l1_norm, l1_w_in,
           l1_conv_w, l1_conv_b, l1_filt_w1, l1_filt_b1, l1_filt_w2, l1_filt_b2, l1_filt_w3, l1_filt_b3,
           l1_filt_w4, l1_filt_freq, l1_filt_bias, l1_w_out, final_norm):
    b, seq, d = x.shape
    width = l1_w_out.shape[0]
    n = b * seq
    tm = min(512, seq)
    tq = min(512, seq)
    row = lambda a: a.reshape(1, -1).astype(F32)
    col = lambda a: a.reshape(-1, 1).astype(F32)

    s1, s2, s3 = Q_RANK, Q_RANK + KV_RANK, Q_RANK + KV_RANK + ROPE
    kpe_w = l0_w_in[:, s2:s3]
    win = jnp.concatenate([l0_w_in[:, :s2], kpe_w, kpe_w, l0_w_in[:, s3:]], axis=1).astype(BF16)
    wuq3 = l0_w_uq.reshape(Q_RANK, HEADS, QK_DIM)
    wuq = jnp.concatenate([wuq3, wuq3[:, :, NOPE:]], axis=2).reshape(Q_RANK, -1).astype(BF16)
    wukv3 = l0_w_ukv.reshape(KV_RANK, HEADS, NOPE + VDIM)
    wukv = jnp.concatenate([wukv3[:, :, :NOPE].reshape(KV_RANK, -1),
                            wukv3[:, :, NOPE:].reshape(KV_RANK, -1)], axis=1).astype(BF16)
    cos_t, sin_t = _rope_tables(seq)

    q, k, v, g = _mla_in(x.reshape(n, d), row(l0_norm), win, row(l0_q_norm), wuq, row(l0_kv_norm),
                         wukv, cos_t, sin_t, seq, tm)
    h1, h1n = _attention(q.reshape(b, seq, -1), k.reshape(b, seq, -1), v.reshape(b, seq, -1),
                         g.reshape(b, seq, -1), x, l0_w_out.astype(BF16), row(l1_norm), tq)

    nu = 3 * width
    nj = seq // LANES
    cw = jnp.concatenate([l1_conv_w, l1_conv_b[None, :], jnp.zeros((SUBLANES - 4, nu), F32)],
                         axis=0).astype(F32)
    w_in1 = l1_w_in.astype(BF16)
    parts = 2
    wp = width // parts
    uts = []
    for part in range(parts):
        take = lambda a: jnp.concatenate([a[:, grp * width + part * wp:grp * width + (part + 1) * wp]
                                          for grp in range(3)], axis=1)
        u = _hyena_in(h1n, take(w_in1), take(cw), 3 * wp, min(512, seq))
        uts.append(u.reshape(b, nj, LANES, 3 * wp).transpose(3, 1, 0, 2))

    zcat, tcat, deltas = _filter_tables(seq, width)
    w4t = l1_filt_w4.T.reshape(2, 2, width, FILT_HIDDEN).astype(F32)
    w1t = jnp.pad(l1_filt_w1.T.astype(F32), ((0, 0), (0, FILT_HIDDEN - POS_EMB)))
    kf = _filters(zcat, w1t, col(l1_filt_b1), l1_filt_w2.T.astype(F32), col(l1_filt_b2),
                  l1_filt_w3.T.astype(F32), col(l1_filt_b3), col(l1_filt_freq), w4t, tcat, deltas,
                  min(256, width))
    kf = kf.reshape(2, width, 2 * seq // LANES, LANES)

    par = jnp.concatenate([l1_filt_bias.astype(F32), jnp.zeros((SUBLANES - 2, width), F32)], axis=0)
    par = jnp.broadcast_to(par.T[:, :, None], (width, SUBLANES, LANES))

    z_parts = []
    for part in range(parts):
        z2t = _long_conv(uts[part], kf, par, part * wp, 32)
        z_parts.append(z2t.transpose(2, 1, 3, 0).reshape(n, wp))
    out = _hyena_out(z_parts, h1n.reshape(n, d), h1.reshape(n, d), w_in1[:, nu:], l1_w_out.astype(BF16),
                     row(final_norm), min(1024, seq))
    return out.reshape(b, seq, d)
```

```python
import functools
import math

import jax
import jax.numpy as jnp
import numpy as np
from jax import lax
from jax.experimental import pallas as pl
from jax.experimental.pallas import tpu as pltpu

RMS_EPS = 1e-6
HEADS = 8
NOPE = 128
ROPE = 64
VDIM = 128
Q_RANK = 384
KV_RANK = 256
ROPE_THETA = 10000.0
QK_DIM = NOPE + ROPE

POS_EMB = 33
POS_BANDS = (POS_EMB - 1) // 2
FILT_HIDDEN = 64
MIN_DECAY = math.log(1e-2) / 1.5
MAX_DECAY = math.log(1e-2) / 0.3

LANES = 128
SUBLANES = 8
MXU_N = 256
CONV_SLOTS = 2
VMEM_LIMIT = 60 * 1024 * 1024

F32 = jnp.float32
BF16 = jnp.bfloat16


def _rms(x, g):
    return x * lax.rsqrt(jnp.mean(x * x, axis=-1, keepdims=True) + RMS_EPS) * g


def _dot(a, b):
    return jnp.dot(a, b, preferred_element_type=F32)


def _params(sem):
    return pltpu.CompilerParams(dimension_semantics=sem, vmem_limit_bytes=VMEM_LIMIT)


def _mla_in_kernel(x_ref, g0_ref, win_ref, qn_ref, wuq_ref, kvn_ref, wukv_ref,
                   cos_ref, sin_ref, q_ref, k_ref, v_ref, g_ref):
    scale = QK_DIM ** -0.5 * math.log2(math.e)
    xn = _rms(x_ref[...], g0_ref[...]).astype(BF16)
    proj = _dot(xn, win_ref[...])
    c_q = proj[:, :Q_RANK]
    c_kv = proj[:, Q_RANK:Q_RANK + KV_RANK]
    o = Q_RANK + KV_RANK
    kpe = proj[:, o:o + LANES]
    gate = proj[:, o + LANES:]
    cos = cos_ref[...]
    sin = sin_ref[...]

    def rope(pe):
        return pe * cos + pltpu.roll(pe, ROPE // 2, axis=1) * sin

    cqn = _rms(c_q, qn_ref[...]).astype(BF16)
    qf = _dot(cqn, wuq_ref[...])
    for h in range(HEADS):
        nope = qf[:, 2 * LANES * h:2 * LANES * h + LANES]
        pe = rope(qf[:, 2 * LANES * h + LANES:2 * LANES * (h + 1)])
        q_ref[:, 2 * LANES * h:2 * LANES * h + LANES] = (nope * scale).astype(BF16)
        q_ref[:, 2 * LANES * h + LANES:2 * LANES * (h + 1)] = (pe * scale).astype(BF16)

    ckvn = _rms(c_kv, kvn_ref[...]).astype(BF16)
    kv = _dot(ckvn, wukv_ref[...])
    kpe_r = rope(kpe).astype(BF16)
    for h in range(HEADS):
        k_ref[:, 2 * LANES * h:2 * LANES * h + LANES] = kv[:, LANES * h:LANES * (h + 1)].astype(BF16)
        k_ref[:, 2 * LANES * h + LANES:2 * LANES * (h + 1)] = kpe_r
    ones_col = (lax.broadcasted_iota(jnp.int32, (kv.shape[0], LANES), 1) == 0).astype(BF16)
    for h in range(HEADS):
        v_ref[:, 2 * LANES * h:2 * LANES * h + LANES] = kv[:, HEADS * NOPE + VDIM * h:
                                                           HEADS * NOPE + VDIM * (h + 1)].astype(BF16)
        v_ref[:, 2 * LANES * h + LANES:2 * LANES * (h + 1)] = ones_col
    g_ref[...] = (gate * jax.nn.sigmoid(gate)).astype(BF16)


def _mla_in(x2d, g0, win, qn, wuq, kvn, wukv, cos_t, sin_t, seq, tm):
    n, d = x2d.shape
    nt = seq // tm
    full = lambda a: pl.BlockSpec(a.shape, lambda i: (0,) * a.ndim)
    return pl.pallas_call(
        _mla_in_kernel,
        grid=(n // tm,),
        in_specs=[pl.BlockSpec((tm, d), lambda i: (i, 0)), full(g0), full(win), full(qn), full(wuq),
                  full(kvn), full(wukv),
                  pl.BlockSpec((tm, LANES), lambda i: (i % nt, 0)),
                  pl.BlockSpec((tm, LANES), lambda i: (i % nt, 0))],
        out_specs=[pl.BlockSpec((tm, HEADS * 2 * LANES), lambda i: (i, 0)),
                   pl.BlockSpec((tm, HEADS * 2 * LANES), lambda i: (i, 0)),
                   pl.BlockSpec((tm, HEADS * 2 * LANES), lambda i: (i, 0)),
                   pl.BlockSpec((tm, HEADS * VDIM), lambda i: (i, 0))],
        out_shape=[jax.ShapeDtypeStruct((n, HEADS * 2 * LANES), BF16),
                   jax.ShapeDtypeStruct((n, HEADS * 2 * LANES), BF16),
                   jax.ShapeDtypeStruct((n, HEADS * 2 * LANES), BF16),
                   jax.ShapeDtypeStruct((n, HEADS * VDIM), BF16)],
        compiler_params=_params(("parallel",)),
        name="mla_in",
    )(x2d, g0, win, qn, wuq, kvn, wukv, cos_t, sin_t)


def _attn_kernel(q_ref, k_ref, v_ref, g_ref, x_ref, wout_ref, g1_ref, h_ref, hn_ref, o_ref):
    for h in range(HEADS):
        qh = q_ref[0, :, 2 * LANES * h:2 * LANES * (h + 1)]
        kh = k_ref[0, :, 2 * LANES * h:2 * LANES * (h + 1)]
        s = lax.dot_general(qh, kh, (((1,), (1,)), ((), ())), preferred_element_type=F32)
        m = jnp.max(s, axis=-1, keepdims=True)
        p = jnp.exp2((s - m).astype(BF16))
        ov = _dot(p, v_ref[0, :, 2 * LANES * h:2 * LANES * (h + 1)])
        o_ref[:, VDIM * h:VDIM * (h + 1)] = ov[:, :VDIM] / ov[:, VDIM:VDIM + 1]
    og = (o_ref[...] * g_ref[0].astype(F32)).astype(BF16)
    h1 = x_ref[0] + _dot(og, wout_ref[...])
    h_ref[0] = h1
    hn_ref[0] = _rms(h1, g1_ref[...]).astype(BF16)


def _attention(q, k, v, g, x, wout, g1, tq):
    b, seq, d = x.shape
    qspec = lambda w: pl.BlockSpec((1, tq, w), lambda i, j: (i, j, 0))
    kspec = lambda w: pl.BlockSpec((1, seq, w), lambda i, j: (i, 0, 0))
    full = lambda a: pl.BlockSpec(a.shape, lambda i, j: (0,) * a.ndim)
    return pl.pallas_call(
        _attn_kernel,
        grid=(b, seq // tq),
        in_specs=[qspec(q.shape[-1]), kspec(k.shape[-1]), kspec(v.shape[-1]), qspec(g.shape[-1]),
                  qspec(d), full(wout), full(g1)],
        out_specs=[qspec(d), qspec(d)],
        out_shape=[jax.ShapeDtypeStruct((b, seq, d), F32), jax.ShapeDtypeStruct((b, seq, d), BF16)],
        scratch_shapes=[pltpu.VMEM((tq, HEADS * VDIM), F32)],
        compiler_params=_params(("parallel", "arbitrary")),
        name="mla_attn",
    )(q, k, v, g, x, wout, g1)


def _hyena_in_kernel(hn_ref, w_ref, cw_ref, o_ref, *, rc):
    seq, tn = hn_ref.shape[1], w_ref.shape[1]
    halo = 16
    w0, w1, w2, cb = cw_ref[0:1, :], cw_ref[1:2, :], cw_ref[2:3, :], cw_ref[3:4, :]
    rows = lax.broadcasted_iota(jnp.int32, (halo, tn), 0)
    nchunk = seq // rc
    bounds = [(max(rc * r - halo, 0), min(rc * (r + 1) + halo, seq)) for r in range(nchunk)]
    project = lambda r: _dot(hn_ref[0, bounds[r][0]:bounds[r][1], :], w_ref[...])
    nxt_proj = project(0)
    for r in range(nchunk):
        lo, hi = bounds[r]
        n = hi - lo
        proj = nxt_proj
        if r + 1 < nchunk:
            nxt_proj = project(r + 1)
        prev = pltpu.roll(proj, 1, axis=0)
        nxt = pltpu.roll(proj, n - 1, axis=0)
        out = cb + w0 * prev + w1 * proj + w2 * nxt
        first = rc * r - lo
        o_ref[0, rc * r:rc * (r + 1), :] = out[first:first + rc].astype(BF16)
        if r == 0:
            top = cb + w0 * jnp.where(rows == 0, 0.0, prev[:halo]) + w1 * proj[:halo] + w2 * nxt[:halo]
            o_ref[0, :halo, :] = top.astype(BF16)
        if r == nchunk - 1:
            bot = (cb + w0 * prev[n - halo:] + w1 * proj[n - halo:]
                   + w2 * jnp.where(rows == halo - 1, 0.0, nxt[n - halo:]))
            o_ref[0, seq - halo:, :] = bot.astype(BF16)


def _hyena_in(hn, w, cw, tn, rc):
    b, seq, d = hn.shape
    nout = w.shape[1]
    return pl.pallas_call(
        functools.partial(_hyena_in_kernel, rc=rc),
        grid=(b, nout // tn),
        in_specs=[pl.BlockSpec((1, seq, d), lambda i, j: (i, 0, 0)),
                  pl.BlockSpec((d, tn), lambda i, j: (0, j)),
                  pl.BlockSpec((SUBLANES, tn), lambda i, j: (0, j))],
        out_specs=pl.BlockSpec((1, seq, tn), lambda i, j: (i, 0, j)),
        out_shape=jax.ShapeDtypeStruct((b, seq, nout), BF16),
        compiler_params=_params(("parallel", "arbitrary")),
        name="hyena_in",
    )(hn, w, cw)


def _bf16_bits(x):
    bits = lax.bitcast_convert_type(x, jnp.uint32)
    return (bits + jnp.uint32(0x7FFF) + ((bits >> 16) & jnp.uint32(1))) >> 16


def _dot_split(a, b):
    a_hi, b_hi = a.astype(BF16), b.astype(BF16)
    a_lo = (a - a_hi.astype(F32)).astype(BF16)
    b_lo = (b - b_hi.astype(F32)).astype(BF16)
    return _dot(a_hi, b_hi) + _dot(a_hi, b_lo) + _dot(a_lo, b_hi)


def _filter_kernel(z_ref, w1_ref, b1_ref, w2_ref, b2_ref, w3_ref, b3_ref, fr_ref, w4_ref, t_ref, dl_ref,
                   kf_ref, a_ref):
    seq = a_ref.shape[1] // 2
    hi = lax.Precision.HIGHEST

    @pl.when((pl.program_id(0) == 0) & (pl.program_id(1) == 0))
    def _():
        fr = fr_ref[...]
        a = jnp.sin(fr * (jnp.dot(w1_ref[...], z_ref[...], precision=hi, preferred_element_type=F32)
                          + b1_ref[...]))
        a = jnp.sin(fr * (jnp.dot(w2_ref[...], a, precision=hi, preferred_element_type=F32) + b2_ref[...]))
        a = jnp.sin(fr * (jnp.dot(w3_ref[...], a, precision=hi, preferred_element_type=F32) + b3_ref[...]))
        a_ref[...] = a

    bwd = _dot_split(w4_ref[0, 1], a_ref[:, :seq])
    fwd = _dot_split(w4_ref[0, 0], a_ref[:, seq:])
    decay = jnp.exp(-(dl_ref[...] * t_ref[...]))
    kf = jnp.concatenate([bwd, fwd], axis=1) * decay
    bits = _bf16_bits(kf)
    kf_ref[0] = bits | (pltpu.roll(bits, 1, axis=1) << 16)


def _filters(zcat, w1t, b1, w2t, b2, w3t, b3, fr, w4t, tcat, deltas, tc):
    _, _, width, hid = w4t.shape
    two_l = zcat.shape[1]
    full = lambda a: pl.BlockSpec(a.shape, lambda n, i: (0,) * a.ndim)
    return pl.pallas_call(
        _filter_kernel,
        grid=(2, width // tc),
        in_specs=[full(zcat), full(w1t), full(b1), full(w2t), full(b2), full(w3t), full(b3), full(fr),
                  pl.BlockSpec((1, 2, tc, hid), lambda n, i: (n, 0, i, 0)),
                  full(tcat), pl.BlockSpec((tc, 1), lambda n, i: (i, 0))],
        out_specs=pl.BlockSpec((1, tc, two_l), lambda n, i: (n, i, 0)),
        out_shape=jax.ShapeDtypeStruct((2, width, two_l), jnp.uint32),
        scratch_shapes=[pltpu.VMEM((hid, two_l), F32)],
        compiler_params=_params(("arbitrary", "arbitrary")),
        name="hyena_filters",
    )(zcat, w1t, b1, w2t, b2, w3t, b3, fr, w4t, tcat, deltas)


def _conv_kernel(x1_ref, x2_ref, v_ref, kf1_ref, kf2_ref, kfn_ref, par_ref, o_ref,
                 lhs_ref, rhs1_ref, rhs2_ref, z1_ref, *, ct):
    nj, nb = v_ref.shape[1], v_ref.shape[2]
    ni = nj // 2
    ntile = kf1_ref.shape[1]
    ng = ntile - 2
    pad = 2 * ni - 2
    rows_per_word_tile = 2 * SUBLANES

    @pl.when(pl.program_id(0) == 0)
    def _():
        lhs_ref[...] = jnp.zeros(lhs_ref.shape, lhs_ref.dtype)

    diff = (lax.broadcasted_iota(jnp.int32, (SUBLANES, LANES), 1)
            - 2 * lax.broadcasted_iota(jnp.int32, (SUBLANES, LANES), 0))
    nroll = LANES // rows_per_word_tile

    def store_lhs(s, j, slab):
        for i in range(ni):
            jj = j + pad - 2 * i
            lhs_ref[s, nb * i:nb * (i + 1), LANES * jj:LANES * (jj + 1)] = slab

    def build_rhs(kf_ref, c, rhs_ref):
        prev = None
        for x in range(ntile - 1, -1, -1):
            tile = jnp.broadcast_to(kf_ref[c, x:x + 1, :], (SUBLANES, LANES))
            cur = [pltpu.roll(tile, rows_per_word_tile * e, axis=1, stride=2, stride_axis=0)
                   for e in range(nroll)]
            if prev is not None:
                g = ng - x
                for e in range(nroll):
                    words = jnp.where(diff < rows_per_word_tile * e, cur[e], prev[e])
                    blk = pltpu.bitcast(words, BF16)
                    r = LANES * g + rows_per_word_tile * e
                    if g < ng:
                        rhs_ref[r:r + rows_per_word_tile, LANES:2 * LANES] = blk
                    if g >= 1:
                        rhs_ref[r - LANES:r - LANES + rows_per_word_tile, 0:LANES] = blk
            prev = cur

    def piece(y, j):
        i, half = divmod(j, 2)
        return y[nb * i:nb * (i + 1), LANES * half:LANES * (half + 1)]

    nslot = lhs_ref.shape[0]
    slots = tuple(range(nslot))
    ngroup = ct // nslot

    @pl.when(pl.program_id(0) == 0)
    def _():
        for s in slots:
            build_rhs(kf1_ref, s, rhs1_ref.at[s])

    def channel_group(p, last):
        cs = [nslot * p + s for s in slots]
        bias1 = [par_ref[c, 0:1, :] for c in cs]
        bias2 = [par_ref[c, 1:2, :] for c in cs]
        for s, c in zip(slots, cs):
            for j in range(nj):
                store_lhs(s, j, v_ref[c, j])
        y1 = [_dot(lhs_ref[s], rhs1_ref[s]) for s in slots]
        for s, c in zip(slots, cs):
            build_rhs(kf2_ref, c, rhs2_ref.at[s])
        for s, c in zip(slots, cs):
            for j in range(nj):
                z1 = x1_ref[c, j].astype(F32) * (piece(y1[s], j) + bias1[s] * v_ref[c, j].astype(F32))
                z1_ref[s, j] = z1
                store_lhs(s, j, z1.astype(BF16))
        y2 = [_dot(lhs_ref[s], rhs2_ref[s]) for s in slots]
        for s, c in zip(slots, cs):
            if last:
                build_rhs(kfn_ref, s, rhs1_ref.at[s])
            else:
                build_rhs(kf1_ref, c + nslot, rhs1_ref.at[s])
        for s, c in zip(slots, cs):
            for j in range(nj):
                z2 = x2_ref[c, j].astype(F32) * (piece(y2[s], j) + bias2[s] * z1_ref[s, j])
                o_ref[c, j] = z2.astype(BF16)

    def body(p, carry):
        channel_group(p, last=False)
        return carry

    lax.fori_loop(0, ngroup - 1, body, 0)
    channel_group(ngroup - 1, last=True)


def _long_conv(ut, kf, par, first_channel, ct):
    _, nj, nb, _ = ut.shape
    wp = ut.shape[0] // 3
    seq = nj * LANES
    kdim = 2 * seq - MXU_N
    nblk = wp // ct
    base = first_channel // ct
    ntile = kf.shape[2]
    uspec = lambda o: pl.BlockSpec((ct, nj, nb, LANES), lambda i: (i + o * nblk, 0, 0, 0))
    kspec = lambda n: pl.BlockSpec((None, ct, ntile, LANES), lambda i: (n, i + base, 0, 0))
    per = ct // CONV_SLOTS
    last_blk = kf.shape[1] // CONV_SLOTS - 1
    nspec = pl.BlockSpec((None, CONV_SLOTS, ntile, LANES),
                         lambda i: (0, jnp.minimum((i + base + 1) * per, last_blk), 0, 0))
    return pl.pallas_call(
        functools.partial(_conv_kernel, ct=ct),
        grid=(nblk,),
        in_specs=[uspec(0), uspec(1), uspec(2), kspec(0), kspec(1), nspec,
                  pl.BlockSpec((ct, SUBLANES, LANES), lambda i: (i + base, 0, 0))],
        out_specs=pl.BlockSpec((ct, nj, nb, LANES), lambda i: (i, 0, 0, 0)),
        out_shape=jax.ShapeDtypeStruct((wp, nj, nb, LANES), BF16),
        scratch_shapes=[pltpu.VMEM((CONV_SLOTS, nj // 2 * nb, kdim), BF16),
                        pltpu.VMEM((CONV_SLOTS, kdim, MXU_N), BF16),
                        pltpu.VMEM((CONV_SLOTS, kdim, MXU_N), BF16),
                        pltpu.VMEM((CONV_SLOTS, nj, nb, LANES), F32)],
        compiler_params=_params(("arbitrary",)),
        name="hyena_long_conv",
    )(ut, ut, ut, kf, kf, kf, par)


def _out_kernel(*refs):
    *z_refs, hn_ref, h_ref, wg_ref, w_ref, gf_ref, o_ref = refs
    gate = _dot(hn_ref[...], wg_ref[...])
    z = jnp.concatenate([r[...] for r in z_refs], axis=1).astype(F32)
    zg = (z * (gate * jax.nn.sigmoid(gate))).astype(BF16)
    h2 = h_ref[...] + _dot(zg, w_ref[...])
    o_ref[...] = _rms(h2, gf_ref[...])


def _hyena_out(z_parts, hn2d, h2d, wg, w, gf, tm):
    n, d = h2d.shape
    row = lambda w_: pl.BlockSpec((tm, w_), lambda i: (i, 0))
    full = lambda a: pl.BlockSpec(a.shape, lambda i: (0,) * a.ndim)
    return pl.pallas_call(
        _out_kernel,
        grid=(n // tm,),
        in_specs=[row(z.shape[1]) for z in z_parts] + [row(d), row(d), full(wg), full(w), full(gf)],
        out_specs=row(d),
        out_shape=jax.ShapeDtypeStruct((n, d), F32),
        compiler_params=_params(("parallel",)),
        name="hyena_out",
    )(*z_parts, hn2d, h2d, wg, w, gf)


def _rope_tables(seq):
    inv = 1.0 / (ROPE_THETA ** (jnp.arange(0, ROPE, 2, dtype=F32) / ROPE))
    ang = jnp.arange(seq, dtype=F32)[:, None] * inv[None, :]
    cos, sin = jnp.cos(ang), jnp.sin(ang)
    zero = jnp.zeros((seq, LANES - ROPE), F32)
    return (jnp.concatenate([cos, cos, zero], axis=1), jnp.concatenate([-sin, sin, zero], axis=1))


def _filter_tables(seq, width):
    t = np.linspace(0.0, 1.0, seq)[:, None]
    w = 2.0 * math.pi * np.arange(seq) / seq
    bands = np.linspace(1e-4, POS_BANDS - 1, POS_BANDS)
    fw = w[:, None] * bands[None, :]
    z = np.concatenate([t, np.cos(fw), -np.sin(fw)], axis=-1)
    deltas = np.abs(np.linspace(MIN_DECAY, MAX_DECAY, width))
    idx = np.abs(np.arange(2 * seq) - seq)
    idx[0] = 0
    zcat = np.pad(z[idx].T, ((0, FILT_HIDDEN - POS_EMB), (0, 0)))
    tcat = t[idx].T.copy()
    tcat[:, 0] = 1e4
    return zcat.astype(np.float32), tcat.astype(np.float32), deltas[:, None].astype(np.float32)


def kernel(x, l0_norm, l0_w_in, l0_q_norm, l0_w_uq, l0_kv_norm, l0_w_ukv, l0_w_out, l1_norm, l1_w_in,
           l1_conv_w, l1_conv_b, l1_filt_w1, l1_filt_b1, l1_filt_w2, l1_filt_b2, l1_filt_w3, l1_filt_b3,
           l1_filt_w4, l1_filt_freq, l1_filt_bias, l1_w_out, final_norm):
    b, seq, d = x.shape
    width = l1_w_out.shape[0]
    n = b * seq
    tm = min(512, seq)
    tq = min(512, seq)
    row = lambda a: a.reshape(1, -1).astype(F32)
    col = lambda a: a.reshape(-1, 1).astype(F32)

    s1, s2, s3 = Q_RANK, Q_RANK + KV_RANK, Q_RANK + KV_RANK + ROPE
    kpe_w = l0_w_in[:, s2:s3]
    win = jnp.concatenate([l0_w_in[:, :s2], kpe_w, kpe_w, l0_w_in[:, s3:]], axis=1).astype(BF16)
    wuq3 = l0_w_uq.reshape(Q_RANK, HEADS, QK_DIM)
    wuq = jnp.concatenate([wuq3, wuq3[:, :, NOPE:]], axis=2).reshape(Q_RANK, -1).astype(BF16)
    wukv3 = l0_w_ukv.reshape(KV_RANK, HEADS, NOPE + VDIM)
    wukv = jnp.concatenate([wukv3[:, :, :NOPE].reshape(KV_RANK, -1),
                            wukv3[:, :, NOPE:].reshape(KV_RANK, -1)], axis=1).astype(BF16)
    cos_t, sin_t = _rope_tables(seq)

    q, k, v, g = _mla_in(x.reshape(n, d), row(l0_norm), win, row(l0_q_norm), wuq, row(l0_kv_norm),
                         wukv, cos_t, sin_t, seq, tm)
    h1, h1n = _attention(q.reshape(b, seq, -1), k.reshape(b, seq, -1), v.reshape(b, seq, -1),
                         g.reshape(b, seq, -1), x, l0_w_out.astype(BF16), row(l1_norm), tq)

    nu = 3 * width
    nj = seq // LANES
    cw = jnp.concatenate([l1_conv_w, l1_conv_b[None, :], jnp.zeros((SUBLANES - 4, nu), F32)],
                         axis=0).astype(F32)
    w_in1 = l1_w_in.astype(BF16)
    parts = 2
    wp = width // parts
    uts = []
    for part in range(parts):
        take = lambda a: jnp.concatenate([a[:, grp * width + part * wp:grp * width + (part + 1) * wp]
                                          for grp in range(3)], axis=1)
        u = _hyena_in(h1n, take(w_in1), take(cw), 3 * wp, min(512, seq))
        uts.append(u.reshape(b, nj, LANES, 3 * wp).transpose(3, 1, 0, 2))

    zcat, tcat, deltas = _filter_tables(seq, width)
    w4t = l1_filt_w4.T.reshape(2, 2, width, FILT_HIDDEN).astype(F32)
    w1t = jnp.pad(l1_filt_w1.T.astype(F32), ((0, 0), (0, FILT_HIDDEN - POS_EMB)))
    kf = _filters(zcat, w1t, col(l1_filt_b1), l1_filt_w2.T.astype(F32), col(l1_filt_b2),
                  l1_filt_w3.T.astype(F32), col(l1_filt_b3), col(l1_filt_freq), w4t, tcat, deltas,
                  min(256, width))
    kf = kf.reshape(2, width, 2 * seq // LANES, LANES)

    par = jnp.concatenate([l1_filt_bias.astype(F32), jnp.zeros((SUBLANES - 2, width), F32)], axis=0)
    par = jnp.broadcast_to(par.T[:, :, None], (width, SUBLANES, LANES))

    z_parts = []
    for part in range(parts):
        z2t = _long_conv(uts[part], kf, par, part * wp, 32)
        z_parts.append(z2t.transpose(2, 1, 3, 0).reshape(n, wp))
    out = _hyena_out(z_parts, h1n.reshape(n, d), h1.reshape(n, d), w_in1[:, nu:], l1_w_out.astype(BF16),
                     row(final_norm), min(1024, seq))
    return out.reshape(b, seq, d)
```

```python
import functools
import math

import jax
import jax.numpy as jnp
import numpy as np
from jax import lax
from jax.experimental import pallas as pl
from jax.experimental.pallas import tpu as pltpu

RMS_EPS = 1e-6
HEADS = 8
NOPE = 128
ROPE = 64
VDIM = 128
Q_RANK = 384
KV_RANK = 256
ROPE_THETA = 10000.0
QK_DIM = NOPE + ROPE

POS_EMB = 33
POS_BANDS = (POS_EMB - 1) // 2
FILT_HIDDEN = 64
MIN_DECAY = math.log(1e-2) / 1.5
MAX_DECAY = math.log(1e-2) / 0.3

LANES = 128
SUBLANES = 8
MXU_N = 256
CONV_SLOTS = 2
VMEM_LIMIT = 60 * 1024 * 1024

TOKEN_TILE = 512
QUERY_TILE = 512
CONV_ROW_CHUNK = 512
OUT_TILE = 1024
FILTER_CHANNELS = 256
CONV_CHANNELS = 32
LAYER1_PARTS = 2

F32 = jnp.float32
BF16 = jnp.bfloat16


def _rms(x, g):
    return x * lax.rsqrt(jnp.mean(x * x, axis=-1, keepdims=True) + RMS_EPS) * g


def _dot(a, b):
    return jnp.dot(a, b, preferred_element_type=F32)


def _params(sem):
    return pltpu.CompilerParams(dimension_semantics=sem, vmem_limit_bytes=VMEM_LIMIT)


def _mla_in_kernel(x_ref, g0_ref, win_ref, qn_ref, wuq_ref, kvn_ref, wukv_ref,
                   cos_ref, sin_ref, q_ref, k_ref, v_ref, g_ref):
    scale = QK_DIM ** -0.5 * math.log2(math.e)
    xn = _rms(x_ref[...], g0_ref[...]).astype(BF16)
    proj = _dot(xn, win_ref[...])
    c_q = proj[:, :Q_RANK]
    c_kv = proj[:, Q_RANK:Q_RANK + KV_RANK]
    o = Q_RANK + KV_RANK
    kpe = proj[:, o:o + LANES]
    gate = proj[:, o + LANES:]
    cos = cos_ref[...]
    sin = sin_ref[...]

    def rope(pe):
        return pe * cos + pltpu.roll(pe, ROPE // 2, axis=1) * sin

    cqn = _rms(c_q, qn_ref[...]).astype(BF16)
    qf = _dot(cqn, wuq_ref[...])
    for h in range(HEADS):
        nope = qf[:, 2 * LANES * h:2 * LANES * h + LANES]
        pe = rope(qf[:, 2 * LANES * h + LANES:2 * LANES * (h + 1)])
        q_ref[:, 2 * LANES * h:2 * LANES * h + LANES] = (nope * scale).astype(BF16)
        q_ref[:, 2 * LANES * h + LANES:2 * LANES * (h + 1)] = (pe * scale).astype(BF16)

    ckvn = _rms(c_kv, kvn_ref[...]).astype(BF16)
    kv = _dot(ckvn, wukv_ref[...])
    kpe_r = rope(kpe).astype(BF16)
    for h in range(HEADS):
        k_ref[:, 2 * LANES * h:2 * LANES * h + LANES] = kv[:, LANES * h:LANES * (h + 1)].astype(BF16)
        k_ref[:, 2 * LANES * h + LANES:2 * LANES * (h + 1)] = kpe_r
    ones_col = (lax.broadcasted_iota(jnp.int32, (kv.shape[0], LANES), 1) == 0).astype(BF16)
    for h in range(HEADS):
        v_ref[:, 2 * LANES * h:2 * LANES * h + LANES] = kv[:, HEADS * NOPE + VDIM * h:
                                                           HEADS * NOPE + VDIM * (h + 1)].astype(BF16)
        v_ref[:, 2 * LANES * h + LANES:2 * LANES * (h + 1)] = ones_col
    g_ref[...] = (gate * jax.nn.sigmoid(gate)).astype(BF16)


def _mla_in(x2d, g0, win, qn, wuq, kvn, wukv, cos_t, sin_t, seq, tm):
    n, d = x2d.shape
    nt = seq // tm
    full = lambda a: pl.BlockSpec(a.shape, lambda i: (0,) * a.ndim)
    return pl.pallas_call(
        _mla_in_kernel,
        grid=(n // tm,),
        in_specs=[pl.BlockSpec((tm, d), lambda i: (i, 0)), full(g0), full(win), full(qn), full(wuq),
                  full(kvn), full(wukv),
                  pl.BlockSpec((tm, LANES), lambda i: (i % nt, 0)),
                  pl.BlockSpec((tm, LANES), lambda i: (i % nt, 0))],
        out_specs=[pl.BlockSpec((tm, HEADS * 2 * LANES), lambda i: (i, 0)),
                   pl.BlockSpec((tm, HEADS * 2 * LANES), lambda i: (i, 0)),
                   pl.BlockSpec((tm, HEADS * 2 * LANES), lambda i: (i, 0)),
                   pl.BlockSpec((tm, HEADS * VDIM), lambda i: (i, 0))],
        out_shape=[jax.ShapeDtypeStruct((n, HEADS * 2 * LANES), BF16),
                   jax.ShapeDtypeStruct((n, HEADS * 2 * LANES), BF16),
                   jax.ShapeDtypeStruct((n, HEADS * 2 * LANES), BF16),
                   jax.ShapeDtypeStruct((n, HEADS * VDIM), BF16)],
        compiler_params=_params(("parallel",)),
        name="mla_in",
    )(x2d, g0, win, qn, wuq, kvn, wukv, cos_t, sin_t)


def _attn_kernel(q_ref, k_ref, v_ref, g_ref, x_ref, wout_ref, g1_ref, h_ref, hn_ref, o_ref):
    for h in range(HEADS):
        qh = q_ref[0, :, 2 * LANES * h:2 * LANES * (h + 1)]
        kh = k_ref[0, :, 2 * LANES * h:2 * LANES * (h + 1)]
        s = lax.dot_general(qh, kh, (((1,), (1,)), ((), ())), preferred_element_type=F32)
        m = jnp.max(s, axis=-1, keepdims=True)
        p = jnp.exp2((s - m).astype(BF16))
        ov = _dot(p, v_ref[0, :, 2 * LANES * h:2 * LANES * (h + 1)])
        o_ref[:, VDIM * h:VDIM * (h + 1)] = ov[:, :VDIM] / ov[:, VDIM:VDIM + 1]
    og = (o_ref[...] * g_ref[0].astype(F32)).astype(BF16)
    h1 = x_ref[0] + _dot(og, wout_ref[...])
    h_ref[0] = h1
    hn_ref[0] = _rms(h1, g1_ref[...]).astype(BF16)


def _attention(q, k, v, g, x, wout, g1, tq):
    b, seq, d = x.shape
    qspec = lambda w: pl.BlockSpec((1, tq, w), lambda i, j: (i, j, 0))
    kspec = lambda w: pl.BlockSpec((1, seq, w), lambda i, j: (i, 0, 0))
    full = lambda a: pl.BlockSpec(a.shape, lambda i, j: (0,) * a.ndim)
    return pl.pallas_call(
        _attn_kernel,
        grid=(b, seq // tq),
        in_specs=[qspec(q.shape[-1]), kspec(k.shape[-1]), kspec(v.shape[-1]), qspec(g.shape[-1]),
                  qspec(d), full(wout), full(g1)],
        out_specs=[qspec(d), qspec(d)],
        out_shape=[jax.ShapeDtypeStruct((b, seq, d), F32), jax.ShapeDtypeStruct((b, seq, d), BF16)],
        scratch_shapes=[pltpu.VMEM((tq, HEADS * VDIM), F32)],
        compiler_params=_params(("parallel", "arbitrary")),
        name="mla_attn",
    )(q, k, v, g, x, wout, g1)


def _hyena_in_kernel(hn_ref, w_ref, cw_ref, o_ref, *, rc):
    seq, tn = hn_ref.shape[1], w_ref.shape[1]
    halo = 16
    w0, w1, w2, cb = cw_ref[0:1, :], cw_ref[1:2, :], cw_ref[2:3, :], cw_ref[3:4, :]
    rows = lax.broadcasted_iota(jnp.int32, (halo, tn), 0)
    nchunk = seq // rc
    bounds = [(max(rc * r - halo, 0), min(rc * (r + 1) + halo, seq)) for r in range(nchunk)]
    project = lambda r: _dot(hn_ref[0, bounds[r][0]:bounds[r][1], :], w_ref[...])
    nxt_proj = project(0)
    for r in range(nchunk):
        lo, hi = bounds[r]
        n = hi - lo
        proj = nxt_proj
        if r + 1 < nchunk:
            nxt_proj = project(r + 1)
        prev = pltpu.roll(proj, 1, axis=0)
        nxt = pltpu.roll(proj, n - 1, axis=0)
        out = cb + w0 * prev + w1 * proj + w2 * nxt
        first = rc * r - lo
        o_ref[0, rc * r:rc * (r + 1), :] = out[first:first + rc].astype(BF16)
        if r == 0:
            top = cb + w0 * jnp.where(rows == 0, 0.0, prev[:halo]) + w1 * proj[:halo] + w2 * nxt[:halo]
            o_ref[0, :halo, :] = top.astype(BF16)
        if r == nchunk - 1:
            bot = (cb + w0 * prev[n - halo:] + w1 * proj[n - halo:]
                   + w2 * jnp.where(rows == halo - 1, 0.0, nxt[n - halo:]))
            o_ref[0, seq - halo:, :] = bot.astype(BF16)


def _hyena_in(hn, w, cw, tn, rc):
    b, seq, d = hn.shape
    nout = w.shape[1]
    return pl.pallas_call(
        functools.partial(_hyena_in_kernel, rc=rc),
        grid=(b, nout // tn),
        in_specs=[pl.BlockSpec((1, seq, d), lambda i, j: (i, 0, 0)),
                  pl.BlockSpec((d, tn), lambda i, j: (0, j)),
                  pl.BlockSpec((SUBLANES, tn), lambda i, j: (0, j))],
        out_specs=pl.BlockSpec((1, seq, tn), lambda i, j: (i, 0, j)),
        out_shape=jax.ShapeDtypeStruct((b, seq, nout), BF16),
        compiler_params=_params(("parallel", "arbitrary")),
        name="hyena_in",
    )(hn, w, cw)


def _bf16_bits(x):
    bits = lax.bitcast_convert_type(x, jnp.uint32)
    return (bits + jnp.uint32(0x7FFF) + ((bits >> 16) & jnp.uint32(1))) >> 16


def _dot_split(a, b):
    a_hi, b_hi = a.astype(BF16), b.astype(BF16)
    a_lo = (a - a_hi.astype(F32)).astype(BF16)
    b_lo = (b - b_hi.astype(F32)).astype(BF16)
    return _dot(a_hi, b_hi) + _dot(a_hi, b_lo) + _dot(a_lo, b_hi)


def _filter_kernel(z_ref, w1_ref, b1_ref, w2_ref, b2_ref, w3_ref, b3_ref, fr_ref, w4_ref, t_ref, dl_ref,
                   kf_ref, a_ref):
    seq = a_ref.shape[1] // 2
    hi = lax.Precision.HIGHEST

    @pl.when((pl.program_id(0) == 0) & (pl.program_id(1) == 0))
    def _():
        fr = fr_ref[...]
        a = jnp.sin(fr * (jnp.dot(w1_ref[...], z_ref[...], precision=hi, preferred_element_type=F32)
                          + b1_ref[...]))
        a = jnp.sin(fr * (jnp.dot(w2_ref[...], a, precision=hi, preferred_element_type=F32) + b2_ref[...]))
        a = jnp.sin(fr * (jnp.dot(w3_ref[...], a, precision=hi, preferred_element_type=F32) + b3_ref[...]))
        a_ref[...] = a

    bwd = _dot_split(w4_ref[0, 1], a_ref[:, :seq])
    fwd = _dot_split(w4_ref[0, 0], a_ref[:, seq:])
    decay = jnp.exp(-(dl_ref[...] * t_ref[...]))
    kf = jnp.concatenate([bwd, fwd], axis=1) * decay
    bits = _bf16_bits(kf)
    kf_ref[0] = bits | (pltpu.roll(bits, 1, axis=1) << 16)


def _filters(zcat, w1t, b1, w2t, b2, w3t, b3, fr, w4t, tcat, deltas, tc):
    _, _, width, hid = w4t.shape
    two_l = zcat.shape[1]
    full = lambda a: pl.BlockSpec(a.shape, lambda n, i: (0,) * a.ndim)
    return pl.pallas_call(
        _filter_kernel,
        grid=(2, width // tc),
        in_specs=[full(zcat), full(w1t), full(b1), full(w2t), full(b2), full(w3t), full(b3), full(fr),
                  pl.BlockSpec((1, 2, tc, hid), lambda n, i: (n, 0, i, 0)),
                  full(tcat), pl.BlockSpec((tc, 1), lambda n, i: (i, 0))],
        out_specs=pl.BlockSpec((1, tc, two_l), lambda n, i: (n, i, 0)),
        out_shape=jax.ShapeDtypeStruct((2, width, two_l), jnp.uint32),
        scratch_shapes=[pltpu.VMEM((hid, two_l), F32)],
        compiler_params=_params(("arbitrary", "arbitrary")),
        name="hyena_filters",
    )(zcat, w1t, b1, w2t, b2, w3t, b3, fr, w4t, tcat, deltas)


def _conv_kernel(x1_ref, x2_ref, v_ref, kf1_ref, kf2_ref, kfn_ref, par_ref, o_ref,
                 lhs_ref, rhs1_ref, rhs2_ref, z1_ref, *, ct):
    nj, nb = v_ref.shape[1], v_ref.shape[2]
    ni = nj // 2
    ntile = kf1_ref.shape[1]
    ng = ntile - 2
    pad = 2 * ni - 2
    rows_per_word_tile = 2 * SUBLANES

    @pl.when(pl.program_id(0) == 0)
    def _():
        lhs_ref[...] = jnp.zeros(lhs_ref.shape, lhs_ref.dtype)

    diff = (lax.broadcasted_iota(jnp.int32, (SUBLANES, LANES), 1)
            - 2 * lax.broadcasted_iota(jnp.int32, (SUBLANES, LANES), 0))
    nroll = LANES // rows_per_word_tile

    def store_lhs(s, j, slab):
        for i in range(ni):
            jj = j + pad - 2 * i
            lhs_ref[s, nb * i:nb * (i + 1), LANES * jj:LANES * (jj + 1)] = slab

    def build_rhs(kf_ref, c, rhs_ref):
        prev = None
        for x in range(ntile - 1, -1, -1):
            tile = jnp.broadcast_to(kf_ref[c, x:x + 1, :], (SUBLANES, LANES))
            cur = [pltpu.roll(tile, rows_per_word_tile * e, axis=1, stride=2, stride_axis=0)
                   for e in range(nroll)]
            if prev is not None:
                g = ng - x
                for e in range(nroll):
                    words = jnp.where(diff < rows_per_word_tile * e, cur[e], prev[e])
                    blk = pltpu.bitcast(words, BF16)
                    r = LANES * g + rows_per_word_tile * e
                    if g < ng:
                        rhs_ref[r:r + rows_per_word_tile, LANES:2 * LANES] = blk
                    if g >= 1:
                        rhs_ref[r - LANES:r - LANES + rows_per_word_tile, 0:LANES] = blk
            prev = cur

    def piece(y, j):
        i, half = divmod(j, 2)
        return y[nb * i:nb * (i + 1), LANES * half:LANES * (half + 1)]

    nslot = lhs_ref.shape[0]
    slots = tuple(range(nslot))
    ngroup = ct // nslot

    @pl.when(pl.program_id(0) == 0)
    def _():
        for s in slots:
            build_rhs(kf1_ref, s, rhs1_ref.at[s])

    def channel_group(p, last):
        cs = [nslot * p + s for s in slots]
        bias1 = [par_ref[c, 0:1, :] for c in cs]
        bias2 = [par_ref[c, 1:2, :] for c in cs]
        for s, c in zip(slots, cs):
            for j in range(nj):
                store_lhs(s, j, v_ref[c, j])
        y1 = [_dot(lhs_ref[s], rhs1_ref[s]) for s in slots]
        for s, c in zip(slots, cs):
            build_rhs(kf2_ref, c, rhs2_ref.at[s])
        for s, c in zip(slots, cs):
            for j in range(nj):
                z1 = x1_ref[c, j].astype(F32) * (piece(y1[s], j) + bias1[s] * v_ref[c, j].astype(F32))
                z1_ref[s, j] = z1
                store_lhs(s, j, z1.astype(BF16))
        y2 = [_dot(lhs_ref[s], rhs2_ref[s]) for s in slots]
        for s, c in zip(slots, cs):
            if last:
                build_rhs(kfn_ref, s, rhs1_ref.at[s])
            else:
                build_rhs(kf1_ref, c + nslot, rhs1_ref.at[s])
        for s, c in zip(slots, cs):
            for j in range(nj):
                z2 = x2_ref[c, j].astype(F32) * (piece(y2[s], j) + bias2[s] * z1_ref[s, j])
                o_ref[c, j] = z2.astype(BF16)

    def body(p, carry):
        channel_group(p, last=False)
        return carry

    lax.fori_loop(0, ngroup - 1, body, 0)
    channel_group(ngroup - 1, last=True)


def _long_conv(ut, kf, par, first_channel, ct):
    _, nj, nb, _ = ut.shape
    wp = ut.shape[0] // 3
    seq = nj * LANES
    kdim = 2 * seq - MXU_N
    nblk = wp // ct
    base = first_channel // ct
    ntile = kf.shape[2]
    uspec = lambda o: pl.BlockSpec((ct, nj, nb, LANES), lambda i: (i + o * nblk, 0, 0, 0))
    kspec = lambda n: pl.BlockSpec((None, ct, ntile, LANES), lambda i: (n, i + base, 0, 0))
    per = ct // CONV_SLOTS
    last_blk = kf.shape[1] // CONV_SLOTS - 1
    nspec = pl.BlockSpec((None, CONV_SLOTS, ntile, LANES),
                         lambda i: (0, jnp.minimum((i + base + 1) * per, last_blk), 0, 0))
    return pl.pallas_call(
        functools.partial(_conv_kernel, ct=ct),
        grid=(nblk,),
        in_specs=[uspec(0), uspec(1), uspec(2), kspec(0), kspec(1), nspec,
                  pl.BlockSpec((ct, SUBLANES, LANES), lambda i: (i + base, 0, 0))],
        out_specs=pl.BlockSpec((ct, nj, nb, LANES), lambda i: (i, 0, 0, 0)),
        out_shape=jax.ShapeDtypeStruct((wp, nj, nb, LANES), BF16),
        scratch_shapes=[pltpu.VMEM((CONV_SLOTS, nj // 2 * nb, kdim), BF16),
                        pltpu.VMEM((CONV_SLOTS, kdim, MXU_N), BF16),
                        pltpu.VMEM((CONV_SLOTS, kdim, MXU_N), BF16),
                        pltpu.VMEM((CONV_SLOTS, nj, nb, LANES), F32)],
        compiler_params=_params(("arbitrary",)),
        name="hyena_long_conv",
    )(ut, ut, ut, kf, kf, kf, par)


def _out_kernel(*refs):
    *z_refs, hn_ref, h_ref, wg_ref, w_ref, gf_ref, o_ref = refs
    gate = _dot(hn_ref[...], wg_ref[...])
    z = jnp.concatenate([r[...] for r in z_refs], axis=1).astype(F32)
    zg = (z * (gate * jax.nn.sigmoid(gate))).astype(BF16)
    h2 = h_ref[...] + _dot(zg, w_ref[...])
    o_ref[...] = _rms(h2, gf_ref[...])


def _hyena_out(z_parts, hn2d, h2d, wg, w, gf, tm):
    n, d = h2d.shape
    row = lambda w_: pl.BlockSpec((tm, w_), lambda i: (i, 0))
    full = lambda a: pl.BlockSpec(a.shape, lambda i: (0,) * a.ndim)
    return pl.pallas_call(
        _out_kernel,
        grid=(n // tm,),
        in_specs=[row(z.shape[1]) for z in z_parts] + [row(d), row(d), full(wg), full(w), full(gf)],
        out_specs=row(d),
        out_shape=jax.ShapeDtypeStruct((n, d), F32),
        compiler_params=_params(("parallel",)),
        name="hyena_out",
    )(*z_parts, hn2d, h2d, wg, w, gf)


def _rope_tables(seq):
    inv = 1.0 / (ROPE_THETA ** (jnp.arange(0, ROPE, 2, dtype=F32) / ROPE))
    ang = jnp.arange(seq, dtype=F32)[:, None] * inv[None, :]
    cos, sin = jnp.cos(ang), jnp.sin(ang)
    zero = jnp.zeros((seq, LANES - ROPE), F32)
    return (jnp.concatenate([cos, cos, zero], axis=1), jnp.concatenate([-sin, sin, zero], axis=1))


def _filter_tables(seq, width):
    t = np.linspace(0.0, 1.0, seq)[:, None]
    w = 2.0 * math.pi * np.arange(seq) / seq
    bands = np.linspace(1e-4, POS_BANDS - 1, POS_BANDS)
    fw = w[:, None] * bands[None, :]
    z = np.concatenate([t, np.cos(fw), -np.sin(fw)], axis=-1)
    deltas = np.abs(np.linspace(MIN_DECAY, MAX_DECAY, width))
    idx = np.abs(np.arange(2 * seq) - seq)
    idx[0] = 0
    zcat = np.pad(z[idx].T, ((0, FILT_HIDDEN - POS_EMB), (0, 0)))
    tcat = t[idx].T.copy()
    tcat[:, 0] = 1e4
    return zcat.astype(np.float32), tcat.astype(np.float32), deltas[:, None].astype(np.float32)


def kernel(x, l0_norm, l0_w_in, l0_q_norm, l0_w_uq, l0_kv_norm, l0_w_ukv, l0_w_out, l1_norm, l1_w_in,
           l1_conv_w, l1_conv_b, l1_filt_w1, l1_filt_b1, l1_filt_w2, l1_filt_b2, l1_filt_w3, l1_filt_b3,
           l1_filt_w4, l1_filt_freq, l1_filt_bias, l1_w_out, final_norm):
    b, seq, d = x.shape
    width = l1_w_out.shape[0]
    n = b * seq
    tm = min(TOKEN_TILE, seq)
    tq = min(QUERY_TILE, seq)
    row = lambda a: a.reshape(1, -1).astype(F32)
    col = lambda a: a.reshape(-1, 1).astype(F32)

    s2, s3 = Q_RANK + KV_RANK, Q_RANK + KV_RANK + ROPE
    kpe_w = l0_w_in[:, s2:s3]
    win = jnp.concatenate([l0_w_in[:, :s2], kpe_w, kpe_w, l0_w_in[:, s3:]], axis=1).astype(BF16)
    wuq3 = l0_w_uq.reshape(Q_RANK, HEADS, QK_DIM)
    wuq = jnp.concatenate([wuq3, wuq3[:, :, NOPE:]], axis=2).reshape(Q_RANK, -1).astype(BF16)
    wukv3 = l0_w_ukv.reshape(KV_RANK, HEADS, NOPE + VDIM)
    wukv = jnp.concatenate([wukv3[:, :, :NOPE].reshape(KV_RANK, -1),
                            wukv3[:, :, NOPE:].reshape(KV_RANK, -1)], axis=1).astype(BF16)
    cos_t, sin_t = _rope_tables(seq)

    q, k, v, g = _mla_in(x.reshape(n, d), row(l0_norm), win, row(l0_q_norm), wuq, row(l0_kv_norm),
                         wukv, cos_t, sin_t, seq, tm)
    h1, h1n = _attention(q.reshape(b, seq, -1), k.reshape(b, seq, -1), v.reshape(b, seq, -1),
                         g.reshape(b, seq, -1), x, l0_w_out.astype(BF16), row(l1_norm), tq)

    nu = 3 * width
    nj = seq // LANES
    cw = jnp.concatenate([l1_conv_w, l1_conv_b[None, :], jnp.zeros((SUBLANES - 4, nu), F32)],
                         axis=0).astype(F32)
    w_in1 = l1_w_in.astype(BF16)
    parts = LAYER1_PARTS
    wp = width // parts
    uts = []
    for part in range(parts):
        take = lambda a: jnp.concatenate([a[:, grp * width + part * wp:grp * width + (part + 1) * wp]
                                          for grp in range(3)], axis=1)
        u = _hyena_in(h1n, take(w_in1), take(cw), 3 * wp, min(CONV_ROW_CHUNK, seq))
        uts.append(u.reshape(b, nj, LANES, 3 * wp).transpose(3, 1, 0, 2))

    zcat, tcat, deltas = _filter_tables(seq, width)
    w4t = l1_filt_w4.T.reshape(2, 2, width, FILT_HIDDEN).astype(F32)
    w1t = jnp.pad(l1_filt_w1.T.astype(F32), ((0, 0), (0, FILT_HIDDEN - POS_EMB)))
    kf = _filters(zcat, w1t, col(l1_filt_b1), l1_filt_w2.T.astype(F32), col(l1_filt_b2),
                  l1_filt_w3.T.astype(F32), col(l1_filt_b3), col(l1_filt_freq), w4t, tcat, deltas,
                  min(FILTER_CHANNELS, width))
    kf = kf.reshape(2, width, 2 * seq // LANES, LANES)

    par = jnp.concatenate([l1_filt_bias.astype(F32), jnp.zeros((SUBLANES - 2, width), F32)], axis=0)
    par = jnp.broadcast_to(par.T[:, :, None], (width, SUBLANES, LANES))

    z_parts = []
    for part in range(parts):
        z2t = _long_conv(uts[part], kf, par, part * wp, CONV_CHANNELS)
        z_parts.append(z2t.transpose(2, 1, 3, 0).reshape(n, wp))
    out = _hyena_out(z_parts, h1n.reshape(n, d), h1.reshape(n, d), w_in1[:, nu:], l1_w_out.astype(BF16),
                     row(final_norm), min(OUT_TILE, seq))
    return out.reshape(b, seq, d)
```

```python
import functools
import math

import jax
import jax.numpy as jnp
import numpy as np
from jax import lax
from jax.experimental import pallas as pl
from jax.experimental.pallas import tpu as pltpu

RMS_EPS = 1e-6
HEADS = 8
NOPE = 128
ROPE = 64
VDIM = 128
Q_RANK = 384
KV_RANK = 256
ROPE_THETA = 10000.0
QK_DIM = NOPE + ROPE

POS_EMB = 33
POS_BANDS = (POS_EMB - 1) // 2
FILT_HIDDEN = 64
MIN_DECAY = math.log(1e-2) / 1.5
MAX_DECAY = math.log(1e-2) / 0.3

LANES = 128
SUBLANES = 8
MXU_N = 256
CONV_SLOTS = 2
VMEM_LIMIT = 60 * 1024 * 1024

TOKEN_TILE = 1024
QUERY_TILE = 512
CONV_ROW_CHUNK = 512
OUT_TILE = 1024
FILTER_CHANNELS = 256
CONV_CHANNELS = 32
LAYER1_PARTS = 2

F32 = jnp.float32
BF16 = jnp.bfloat16


def _rms(x, g):
    return x * lax.rsqrt(jnp.mean(x * x, axis=-1, keepdims=True) + RMS_EPS) * g


def _dot(a, b):
    return jnp.dot(a, b, preferred_element_type=F32)


def _params(sem):
    return pltpu.CompilerParams(dimension_semantics=sem, vmem_limit_bytes=VMEM_LIMIT)


def _mla_in_kernel(x_ref, g0_ref, win_ref, qn_ref, wuq_ref, kvn_ref, wukv_ref,
                   cos_ref, sin_ref, q_ref, k_ref, v_ref, g_ref):
    scale = QK_DIM ** -0.5 * math.log2(math.e)
    xn = _rms(x_ref[...], g0_ref[...]).astype(BF16)
    proj = _dot(xn, win_ref[...])
    c_q = proj[:, :Q_RANK]
    c_kv = proj[:, Q_RANK:Q_RANK + KV_RANK]
    o = Q_RANK + KV_RANK
    kpe = proj[:, o:o + LANES]
    gate = proj[:, o + LANES:]
    cos = cos_ref[...]
    sin = sin_ref[...]

    def rope(pe):
        return pe * cos + pltpu.roll(pe, ROPE // 2, axis=1) * sin

    cqn = _rms(c_q, qn_ref[...]).astype(BF16)
    qf = _dot(cqn, wuq_ref[...])
    for h in range(HEADS):
        nope = qf[:, 2 * LANES * h:2 * LANES * h + LANES]
        pe = rope(qf[:, 2 * LANES * h + LANES:2 * LANES * (h + 1)])
        q_ref[:, 2 * LANES * h:2 * LANES * h + LANES] = (nope * scale).astype(BF16)
        q_ref[:, 2 * LANES * h + LANES:2 * LANES * (h + 1)] = (pe * scale).astype(BF16)

    ckvn = _rms(c_kv, kvn_ref[...]).astype(BF16)
    kv = _dot(ckvn, wukv_ref[...])
    kpe_r = rope(kpe).astype(BF16)
    for h in range(HEADS):
        k_ref[:, 2 * LANES * h:2 * LANES * h + LANES] = kv[:, LANES * h:LANES * (h + 1)].astype(BF16)
        k_ref[:, 2 * LANES * h + LANES:2 * LANES * (h + 1)] = kpe_r
    ones_col = (lax.broadcasted_iota(jnp.int32, (kv.shape[0], LANES), 1) == 0).astype(BF16)
    for h in range(HEADS):
        v_ref[:, 2 * LANES * h:2 * LANES * h + LANES] = kv[:, HEADS * NOPE + VDIM * h:
                                                           HEADS * NOPE + VDIM * (h + 1)].astype(BF16)
        v_ref[:, 2 * LANES * h + LANES:2 * LANES * (h + 1)] = ones_col
    g_ref[...] = (gate * jax.nn.sigmoid(gate)).astype(BF16)


def _mla_in(x2d, g0, win, qn, wuq, kvn, wukv, cos_t, sin_t, seq, tm):
    n, d = x2d.shape
    nt = seq // tm
    full = lambda a: pl.BlockSpec(a.shape, lambda i: (0,) * a.ndim)
    return pl.pallas_call(
        _mla_in_kernel,
        grid=(n // tm,),
        in_specs=[pl.BlockSpec((tm, d), lambda i: (i, 0)), full(g0), full(win), full(qn), full(wuq),
                  full(kvn), full(wukv),
                  pl.BlockSpec((tm, LANES), lambda i: (i % nt, 0)),
                  pl.BlockSpec((tm, LANES), lambda i: (i % nt, 0))],
        out_specs=[pl.BlockSpec((tm, HEADS * 2 * LANES), lambda i: (i, 0)),
                   pl.BlockSpec((tm, HEADS * 2 * LANES), lambda i: (i, 0)),
                   pl.BlockSpec((tm, HEADS * 2 * LANES), lambda i: (i, 0)),
                   pl.BlockSpec((tm, HEADS * VDIM), lambda i: (i, 0))],
        out_shape=[jax.ShapeDtypeStruct((n, HEADS * 2 * LANES), BF16),
                   jax.ShapeDtypeStruct((n, HEADS * 2 * LANES), BF16),
                   jax.ShapeDtypeStruct((n, HEADS * 2 * LANES), BF16),
                   jax.ShapeDtypeStruct((n, HEADS * VDIM), BF16)],
        compiler_params=_params(("parallel",)),
        name="mla_in",
    )(x2d, g0, win, qn, wuq, kvn, wukv, cos_t, sin_t)


def _attn_kernel(q_ref, k_ref, v_ref, g_ref, x_ref, wout_ref, g1_ref, h_ref, hn_ref, o_ref):
    for h in range(HEADS):
        qh = q_ref[0, :, 2 * LANES * h:2 * LANES * (h + 1)]
        kh = k_ref[0, :, 2 * LANES * h:2 * LANES * (h + 1)]
        s = lax.dot_general(qh, kh, (((1,), (1,)), ((), ())), preferred_element_type=F32)
        m = jnp.max(s, axis=-1, keepdims=True)
        p = jnp.exp2((s - m).astype(BF16))
        ov = _dot(p, v_ref[0, :, 2 * LANES * h:2 * LANES * (h + 1)])
        o_ref[:, VDIM * h:VDIM * (h + 1)] = ov[:, :VDIM] / ov[:, VDIM:VDIM + 1]
    og = (o_ref[...] * g_ref[0].astype(F32)).astype(BF16)
    h1 = x_ref[0] + _dot(og, wout_ref[...])
    h_ref[0] = h1
    hn_ref[0] = _rms(h1, g1_ref[...]).astype(BF16)


def _attention(q, k, v, g, x, wout, g1, tq):
    b, seq, d = x.shape
    qspec = lambda w: pl.BlockSpec((1, tq, w), lambda i, j: (i, j, 0))
    kspec = lambda w: pl.BlockSpec((1, seq, w), lambda i, j: (i, 0, 0))
    full = lambda a: pl.BlockSpec(a.shape, lambda i, j: (0,) * a.ndim)
    return pl.pallas_call(
        _attn_kernel,
        grid=(b, seq // tq),
        in_specs=[qspec(q.shape[-1]), kspec(k.shape[-1]), kspec(v.shape[-1]), qspec(g.shape[-1]),
                  qspec(d), full(wout), full(g1)],
        out_specs=[qspec(d), qspec(d)],
        out_shape=[jax.ShapeDtypeStruct((b, seq, d), F32), jax.ShapeDtypeStruct((b, seq, d), BF16)],
        scratch_shapes=[pltpu.VMEM((tq, HEADS * VDIM), F32)],
        compiler_params=_params(("parallel", "arbitrary")),
        name="mla_attn",
    )(q, k, v, g, x, wout, g1)


def _hyena_in_kernel(hn_ref, w_ref, cw_ref, o_ref, *, rc):
    seq, tn = hn_ref.shape[1], w_ref.shape[1]
    halo = 16
    w0, w1, w2, cb = cw_ref[0:1, :], cw_ref[1:2, :], cw_ref[2:3, :], cw_ref[3:4, :]
    rows = lax.broadcasted_iota(jnp.int32, (halo, tn), 0)
    nchunk = seq // rc
    bounds = [(max(rc * r - halo, 0), min(rc * (r + 1) + halo, seq)) for r in range(nchunk)]
    project = lambda r: _dot(hn_ref[0, bounds[r][0]:bounds[r][1], :], w_ref[...])
    nxt_proj = project(0)
    for r in range(nchunk):
        lo, hi = bounds[r]
        n = hi - lo
        proj = nxt_proj
        if r + 1 < nchunk:
            nxt_proj = project(r + 1)
        prev = pltpu.roll(proj, 1, axis=0)
        nxt = pltpu.roll(proj, n - 1, axis=0)
        out = cb + w0 * prev + w1 * proj + w2 * nxt
        first = rc * r - lo
        o_ref[0, rc * r:rc * (r + 1), :] = out[first:first + rc].astype(BF16)
        if r == 0:
            top = cb + w0 * jnp.where(rows == 0, 0.0, prev[:halo]) + w1 * proj[:halo] + w2 * nxt[:halo]
            o_ref[0, :halo, :] = top.astype(BF16)
        if r == nchunk - 1:
            bot = (cb + w0 * prev[n - halo:] + w1 * proj[n - halo:]
                   + w2 * jnp.where(rows == halo - 1, 0.0, nxt[n - halo:]))
            o_ref[0, seq - halo:, :] = bot.astype(BF16)


def _hyena_in(hn, w, cw, tn, rc):
    b, seq, d = hn.shape
    nout = w.shape[1]
    return pl.pallas_call(
        functools.partial(_hyena_in_kernel, rc=rc),
        grid=(b, nout // tn),
        in_specs=[pl.BlockSpec((1, seq, d), lambda i, j: (i, 0, 0)),
                  pl.BlockSpec((d, tn), lambda i, j: (0, j)),
                  pl.BlockSpec((SUBLANES, tn), lambda i, j: (0, j))],
        out_specs=pl.BlockSpec((1, seq, tn), lambda i, j: (i, 0, j)),
        out_shape=jax.ShapeDtypeStruct((b, seq, nout), BF16),
        compiler_params=_params(("parallel", "arbitrary")),
        name="hyena_in",
    )(hn, w, cw)


def _bf16_bits(x):
    bits = lax.bitcast_convert_type(x, jnp.uint32)
    return (bits + jnp.uint32(0x7FFF) + ((bits >> 16) & jnp.uint32(1))) >> 16


def _dot_split(a, b):
    a_hi, b_hi = a.astype(BF16), b.astype(BF16)
    a_lo = (a - a_hi.astype(F32)).astype(BF16)
    b_lo = (b - b_hi.astype(F32)).astype(BF16)
    return _dot(a_hi, b_hi) + _dot(a_hi, b_lo) + _dot(a_lo, b_hi)


def _filter_kernel(z_ref, w1_ref, b1_ref, w2_ref, b2_ref, w3_ref, b3_ref, fr_ref, w4_ref, t_ref, dl_ref,
                   kf_ref, a_ref):
    seq = a_ref.shape[1] // 2
    hi = lax.Precision.HIGHEST

    @pl.when((pl.program_id(0) == 0) & (pl.program_id(1) == 0))
    def _():
        fr = fr_ref[...]
        a = jnp.sin(fr * (jnp.dot(w1_ref[...], z_ref[...], precision=hi, preferred_element_type=F32)
                          + b1_ref[...]))
        a = jnp.sin(fr * (jnp.dot(w2_ref[...], a, precision=hi, preferred_element_type=F32) + b2_ref[...]))
        a = jnp.sin(fr * (jnp.dot(w3_ref[...], a, precision=hi, preferred_element_type=F32) + b3_ref[...]))
        a_ref[...] = a

    bwd = _dot_split(w4_ref[0, 1], a_ref[:, :seq])
    fwd = _dot_split(w4_ref[0, 0], a_ref[:, seq:])
    decay = jnp.exp(-(dl_ref[...] * t_ref[...]))
    kf = jnp.concatenate([bwd, fwd], axis=1) * decay
    bits = _bf16_bits(kf)
    kf_ref[0] = bits | (pltpu.roll(bits, 1, axis=1) << 16)


def _filters(zcat, w1t, b1, w2t, b2, w3t, b3, fr, w4t, tcat, deltas, tc):
    _, _, width, hid = w4t.shape
    two_l = zcat.shape[1]
    full = lambda a: pl.BlockSpec(a.shape, lambda n, i: (0,) * a.ndim)
    return pl.pallas_call(
        _filter_kernel,
        grid=(2, width // tc),
        in_specs=[full(zcat), full(w1t), full(b1), full(w2t), full(b2), full(w3t), full(b3), full(fr),
                  pl.BlockSpec((1, 2, tc, hid), lambda n, i: (n, 0, i, 0)),
                  full(tcat), pl.BlockSpec((tc, 1), lambda n, i: (i, 0))],
        out_specs=pl.BlockSpec((1, tc, two_l), lambda n, i: (n, i, 0)),
        out_shape=jax.ShapeDtypeStruct((2, width, two_l), jnp.uint32),
        scratch_shapes=[pltpu.VMEM((hid, two_l), F32)],
        compiler_params=_params(("arbitrary", "arbitrary")),
        name="hyena_filters",
    )(zcat, w1t, b1, w2t, b2, w3t, b3, fr, w4t, tcat, deltas)


def _conv_kernel(x1_ref, x2_ref, v_ref, kf1_ref, kf2_ref, kfn_ref, par_ref, o_ref,
                 lhs_ref, rhs1_ref, rhs2_ref, z1_ref, *, ct):
    nj, nb = v_ref.shape[1], v_ref.shape[2]
    ni = nj // 2
    ntile = kf1_ref.shape[1]
    ng = ntile - 2
    pad = 2 * ni - 2
    rows_per_word_tile = 2 * SUBLANES

    @pl.when(pl.program_id(0) == 0)
    def _():
        lhs_ref[...] = jnp.zeros(lhs_ref.shape, lhs_ref.dtype)

    diff = (lax.broadcasted_iota(jnp.int32, (SUBLANES, LANES), 1)
            - 2 * lax.broadcasted_iota(jnp.int32, (SUBLANES, LANES), 0))
    nroll = LANES // rows_per_word_tile

    def store_lhs(s, j, slab):
        for i in range(ni):
            jj = j + pad - 2 * i
            lhs_ref[s, nb * i:nb * (i + 1), LANES * jj:LANES * (jj + 1)] = slab

    def build_rhs(kf_ref, c, rhs_ref):
        prev = None
        for x in range(ntile - 1, -1, -1):
            tile = jnp.broadcast_to(kf_ref[c, x:x + 1, :], (SUBLANES, LANES))
            cur = [pltpu.roll(tile, rows_per_word_tile * e, axis=1, stride=2, stride_axis=0)
                   for e in range(nroll)]
            if prev is not None:
                g = ng - x
                for e in range(nroll):
                    words = jnp.where(diff < rows_per_word_tile * e, cur[e], prev[e])
                    blk = pltpu.bitcast(words, BF16)
                    r = LANES * g + rows_per_word_tile * e
                    if g < ng:
                        rhs_ref[r:r + rows_per_word_tile, LANES:2 * LANES] = blk
                    if g >= 1:
                        rhs_ref[r - LANES:r - LANES + rows_per_word_tile, 0:LANES] = blk
            prev = cur

    def piece(y, j):
        i, half = divmod(j, 2)
        return y[nb * i:nb * (i + 1), LANES * half:LANES * (half + 1)]

    nslot = lhs_ref.shape[0]
    slots = tuple(range(nslot))
    ngroup = ct // nslot

    @pl.when(pl.program_id(0) == 0)
    def _():
        for s in slots:
            build_rhs(kf1_ref, s, rhs1_ref.at[s])

    def channel_group(p, last):
        cs = [nslot * p + s for s in slots]
        bias1 = [par_ref[c, 0:1, :] for c in cs]
        bias2 = [par_ref[c, 1:2, :] for c in cs]
        for s, c in zip(slots, cs):
            for j in range(nj):
                store_lhs(s, j, v_ref[c, j])
        y1 = [_dot(lhs_ref[s], rhs1_ref[s]) for s in slots]
        for s, c in zip(slots, cs):
            build_rhs(kf2_ref, c, rhs2_ref.at[s])
        for s, c in zip(slots, cs):
            for j in range(nj):
                z1 = x1_ref[c, j].astype(F32) * (piece(y1[s], j) + bias1[s] * v_ref[c, j].astype(F32))
                z1_ref[s, j] = z1
                store_lhs(s, j, z1.astype(BF16))
        y2 = [_dot(lhs_ref[s], rhs2_ref[s]) for s in slots]
        for s, c in zip(slots, cs):
            if last:
                build_rhs(kfn_ref, s, rhs1_ref.at[s])
            else:
                build_rhs(kf1_ref, c + nslot, rhs1_ref.at[s])
        for s, c in zip(slots, cs):
            for j in range(nj):
                z2 = x2_ref[c, j].astype(F32) * (piece(y2[s], j) + bias2[s] * z1_ref[s, j])
                o_ref[c, j] = z2.astype(BF16)

    def body(p, carry):
        channel_group(p, last=False)
        return carry

    lax.fori_loop(0, ngroup - 1, body, 0)
    channel_group(ngroup - 1, last=True)


def _long_conv(ut, kf, par, first_channel, ct):
    _, nj, nb, _ = ut.shape
    wp = ut.shape[0] // 3
    seq = nj * LANES
    kdim = 2 * seq - MXU_N
    nblk = wp // ct
    base = first_channel // ct
    ntile = kf.shape[2]
    uspec = lambda o: pl.BlockSpec((ct, nj, nb, LANES), lambda i: (i + o * nblk, 0, 0, 0))
    kspec = lambda n: pl.BlockSpec((None, ct, ntile, LANES), lambda i: (n, i + base, 0, 0))
    per = ct // CONV_SLOTS
    last_blk = kf.shape[1] // CONV_SLOTS - 1
    nspec = pl.BlockSpec((None, CONV_SLOTS, ntile, LANES),
                         lambda i: (0, jnp.minimum((i + base + 1) * per, last_blk), 0, 0))
    return pl.pallas_call(
        functools.partial(_conv_kernel, ct=ct),
        grid=(nblk,),
        in_specs=[uspec(0), uspec(1), uspec(2), kspec(0), kspec(1), nspec,
                  pl.BlockSpec((ct, SUBLANES, LANES), lambda i: (i + base, 0, 0))],
        out_specs=pl.BlockSpec((ct, nj, nb, LANES), lambda i: (i, 0, 0, 0)),
        out_shape=jax.ShapeDtypeStruct((wp, nj, nb, LANES), BF16),
        scratch_shapes=[pltpu.VMEM((CONV_SLOTS, nj // 2 * nb, kdim), BF16),
                        pltpu.VMEM((CONV_SLOTS, kdim, MXU_N), BF16),
                        pltpu.VMEM((CONV_SLOTS, kdim, MXU_N), BF16),
                        pltpu.VMEM((CONV_SLOTS, nj, nb, LANES), F32)],
        compiler_params=_params(("arbitrary",)),
        name="hyena_long_conv",
    )(ut, ut, ut, kf, kf, kf, par)


def _out_kernel(*refs):
    *z_refs, hn_ref, h_ref, wg_ref, w_ref, gf_ref, o_ref = refs
    gate = _dot(hn_ref[...], wg_ref[...])
    z = jnp.concatenate([r[...] for r in z_refs], axis=1).astype(F32)
    zg = (z * (gate * jax.nn.sigmoid(gate))).astype(BF16)
    h2 = h_ref[...] + _dot(zg, w_ref[...])
    o_ref[...] = _rms(h2, gf_ref[...])


def _hyena_out(z_parts, hn2d, h2d, wg, w, gf, tm):
    n, d = h2d.shape
    row = lambda w_: pl.BlockSpec((tm, w_), lambda i: (i, 0))
    full = lambda a: pl.BlockSpec(a.shape, lambda i: (0,) * a.ndim)
    return pl.pallas_call(
        _out_kernel,
        grid=(n // tm,),
        in_specs=[row(z.shape[1]) for z in z_parts] + [row(d), row(d), full(wg), full(w), full(gf)],
        out_specs=row(d),
        out_shape=jax.ShapeDtypeStruct((n, d), F32),
        compiler_params=_params(("parallel",)),
        name="hyena_out",
    )(*z_parts, hn2d, h2d, wg, w, gf)


def _rope_tables(seq):
    inv = 1.0 / (ROPE_THETA ** (jnp.arange(0, ROPE, 2, dtype=F32) / ROPE))
    ang = jnp.arange(seq, dtype=F32)[:, None] * inv[None, :]
    cos, sin = jnp.cos(ang), jnp.sin(ang)
    zero = jnp.zeros((seq, LANES - ROPE), F32)
    return (jnp.concatenate([cos, cos, zero], axis=1), jnp.concatenate([-sin, sin, zero], axis=1))


def _filter_tables(seq, width):
    t = np.linspace(0.0, 1.0, seq)[:, None]
    w = 2.0 * math.pi * np.arange(seq) / seq
    bands = np.linspace(1e-4, POS_BANDS - 1, POS_BANDS)
    fw = w[:, None] * bands[None, :]
    z = np.concatenate([t, np.cos(fw), -np.sin(fw)], axis=-1)
    deltas = np.abs(np.linspace(MIN_DECAY, MAX_DECAY, width))
    idx = np.abs(np.arange(2 * seq) - seq)
    idx[0] = 0
    zcat = np.pad(z[idx].T, ((0, FILT_HIDDEN - POS_EMB), (0, 0)))
    tcat = t[idx].T.copy()
    tcat[:, 0] = 1e4
    return zcat.astype(np.float32), tcat.astype(np.float32), deltas[:, None].astype(np.float32)


def kernel(x, l0_norm, l0_w_in, l0_q_norm, l0_w_uq, l0_kv_norm, l0_w_ukv, l0_w_out, l1_norm, l1_w_in,
           l1_conv_w, l1_conv_b, l1_filt_w1, l1_filt_b1, l1_filt_w2, l1_filt_b2, l1_filt_w3, l1_filt_b3,
           l1_filt_w4, l1_filt_freq, l1_filt_bias, l1_w_out, final_norm):
    b, seq, d = x.shape
    width = l1_w_out.shape[0]
    n = b * seq
    tm = min(TOKEN_TILE, seq)
    tq = min(QUERY_TILE, seq)
    row = lambda a: a.reshape(1, -1).astype(F32)
    col = lambda a: a.reshape(-1, 1).astype(F32)

    s2, s3 = Q_RANK + KV_RANK, Q_RANK + KV_RANK + ROPE
    kpe_w = l0_w_in[:, s2:s3]
    win = jnp.concatenate([l0_w_in[:, :s2], kpe_w, kpe_w, l0_w_in[:, s3:]], axis=1).astype(BF16)
    wuq3 = l0_w_uq.reshape(Q_RANK, HEADS, QK_DIM)
    wuq = jnp.concatenate([wuq3, wuq3[:, :, NOPE:]], axis=2).reshape(Q_RANK, -1).astype(BF16)
    wukv3 = l0_w_ukv.reshape(KV_RANK, HEADS, NOPE + VDIM)
    wukv = jnp.concatenate([wukv3[:, :, :NOPE].reshape(KV_RANK, -1),
                            wukv3[:, :, NOPE:].reshape(KV_RANK, -1)], axis=1).astype(BF16)
    cos_t, sin_t = _rope_tables(seq)

    q, k, v, g = _mla_in(x.reshape(n, d), row(l0_norm), win, row(l0_q_norm), wuq, row(l0_kv_norm),
                         wukv, cos_t, sin_t, seq, tm)
    h1, h1n = _attention(q.reshape(b, seq, -1), k.reshape(b, seq, -1), v.reshape(b, seq, -1),
                         g.reshape(b, seq, -1), x, l0_w_out.astype(BF16), row(l1_norm), tq)

    nu = 3 * width
    nj = seq // LANES
    cw = jnp.concatenate([l1_conv_w, l1_conv_b[None, :], jnp.zeros((SUBLANES - 4, nu), F32)],
                         axis=0).astype(F32)
    w_in1 = l1_w_in.astype(BF16)
    parts = LAYER1_PARTS
    wp = width // parts
    uts = []
    for part in range(parts):
        take = lambda a: jnp.concatenate([a[:, grp * width + part * wp:grp * width + (part + 1) * wp]
                                          for grp in range(3)], axis=1)
        u = _hyena_in(h1n, take(w_in1), take(cw), 3 * wp, min(CONV_ROW_CHUNK, seq))
        uts.append(u.reshape(b, nj, LANES, 3 * wp).transpose(3, 1, 0, 2))

    zcat, tcat, deltas = _filter_tables(seq, width)
    w4t = l1_filt_w4.T.reshape(2, 2, width, FILT_HIDDEN).astype(F32)
    w1t = jnp.pad(l1_filt_w1.T.astype(F32), ((0, 0), (0, FILT_HIDDEN - POS_EMB)))
    kf = _filters(zcat, w1t, col(l1_filt_b1), l1_filt_w2.T.astype(F32), col(l1_filt_b2),
                  l1_filt_w3.T.astype(F32), col(l1_filt_b3), col(l1_filt_freq), w4t, tcat, deltas,
                  min(FILTER_CHANNELS, width))
    kf = kf.reshape(2, width, 2 * seq // LANES, LANES)

    par = jnp.concatenate([l1_filt_bias.astype(F32), jnp.zeros((SUBLANES - 2, width), F32)], axis=0)
    par = jnp.broadcast_to(par.T[:, :, None], (width, SUBLANES, LANES))

    z_parts = []
    for part in range(parts):
        z2t = _long_conv(uts[part], kf, par, part * wp, CONV_CHANNELS)
        z_parts.append(z2t.transpose(2, 1, 3, 0).reshape(n, wp))
    out = _hyena_out(z_parts, h1n.reshape(n, d), h1.reshape(n, d), w_in1[:, nu:], l1_w_out.astype(BF16),
                     row(final_norm), min(OUT_TILE, seq))
    return out.reshape(b, seq, d)
```

```python
import functools
import math

import jax
import jax.numpy as jnp
import numpy as np
from jax import lax
from jax.experimental import pallas as pl
from jax.experimental.pallas import tpu as pltpu

RMS_EPS = 1e-6
HEADS = 8
NOPE = 128
ROPE = 64
VDIM = 128
Q_RANK = 384
KV_RANK = 256
ROPE_THETA = 10000.0
QK_DIM = NOPE + ROPE

POS_EMB = 33
POS_BANDS = (POS_EMB - 1) // 2
FILT_HIDDEN = 64
MIN_DECAY = math.log(1e-2) / 1.5
MAX_DECAY = math.log(1e-2) / 0.3

LANES = 128
SUBLANES = 8
MXU_N = 256
CONV_SLOTS = 2
VMEM_LIMIT = 60 * 1024 * 1024

TOKEN_TILE = 1024
QUERY_TILE = 512
CONV_ROW_CHUNK = 512
OUT_TILE = 1024
FILTER_CHANNELS = 256
CONV_CHANNELS = 32
LAYER1_PARTS = 2

F32 = jnp.float32
BF16 = jnp.bfloat16


def _rms(x, g):
    return x * lax.rsqrt(jnp.mean(x * x, axis=-1, keepdims=True) + RMS_EPS) * g


def _dot(a, b):
    return jnp.dot(a, b, preferred_element_type=F32)


def _params(sem):
    return pltpu.CompilerParams(dimension_semantics=sem, vmem_limit_bytes=VMEM_LIMIT)


def _mla_in_kernel(x_ref, g0_ref, win_ref, qn_ref, wuq_ref, kvn_ref, wukv_ref,
                   cos_ref, sin_ref, q_ref, k_ref, v_ref, g_ref):
    scale = QK_DIM ** -0.5 * math.log2(math.e)
    xn = _rms(x_ref[...], g0_ref[...]).astype(BF16)
    proj = _dot(xn, win_ref[...])
    c_q = proj[:, :Q_RANK]
    c_kv = proj[:, Q_RANK:Q_RANK + KV_RANK]
    o = Q_RANK + KV_RANK
    kpe = proj[:, o:o + LANES]
    gate = proj[:, o + LANES:]
    cos = cos_ref[...]
    sin = sin_ref[...]

    def rope(pe):
        return pe * cos + pltpu.roll(pe, ROPE // 2, axis=1) * sin

    cqn = _rms(c_q, qn_ref[...]).astype(BF16)
    qf = _dot(cqn, wuq_ref[...])
    for h in range(HEADS):
        nope = qf[:, 2 * LANES * h:2 * LANES * h + LANES]
        pe = rope(qf[:, 2 * LANES * h + LANES:2 * LANES * (h + 1)])
        q_ref[:, 2 * LANES * h:2 * LANES * h + LANES] = (nope * scale).astype(BF16)
        q_ref[:, 2 * LANES * h + LANES:2 * LANES * (h + 1)] = (pe * scale).astype(BF16)

    ckvn = _rms(c_kv, kvn_ref[...]).astype(BF16)
    kv = _dot(ckvn, wukv_ref[...])
    kpe_r = rope(kpe).astype(BF16)
    for h in range(HEADS):
        k_ref[:, 2 * LANES * h:2 * LANES * h + LANES] = kv[:, LANES * h:LANES * (h + 1)].astype(BF16)
        k_ref[:, 2 * LANES * h + LANES:2 * LANES * (h + 1)] = kpe_r
    ones_col = (lax.broadcasted_iota(jnp.int32, (kv.shape[0], LANES), 1) == 0).astype(BF16)
    for h in range(HEADS):
        v_ref[:, 2 * LANES * h:2 * LANES * h + LANES] = kv[:, HEADS * NOPE + VDIM * h:
                                                           HEADS * NOPE + VDIM * (h + 1)].astype(BF16)
        v_ref[:, 2 * LANES * h + LANES:2 * LANES * (h + 1)] = ones_col
    g_ref[...] = (gate * jax.nn.sigmoid(gate)).astype(BF16)


def _mla_in(x2d, g0, win, qn, wuq, kvn, wukv, cos_t, sin_t, seq, tm):
    n, d = x2d.shape
    nt = seq // tm
    full = lambda a: pl.BlockSpec(a.shape, lambda i: (0,) * a.ndim)
    return pl.pallas_call(
        _mla_in_kernel,
        grid=(n // tm,),
        in_specs=[pl.BlockSpec((tm, d), lambda i: (i, 0)), full(g0), full(win), full(qn), full(wuq),
                  full(kvn), full(wukv),
                  pl.BlockSpec((tm, LANES), lambda i: (i % nt, 0)),
                  pl.BlockSpec((tm, LANES), lambda i: (i % nt, 0))],
        out_specs=[pl.BlockSpec((tm, HEADS * 2 * LANES), lambda i: (i, 0)),
                   pl.BlockSpec((tm, HEADS * 2 * LANES), lambda i: (i, 0)),
                   pl.BlockSpec((tm, HEADS * 2 * LANES), lambda i: (i, 0)),
                   pl.BlockSpec((tm, HEADS * VDIM), lambda i: (i, 0))],
        out_shape=[jax.ShapeDtypeStruct((n, HEADS * 2 * LANES), BF16),
                   jax.ShapeDtypeStruct((n, HEADS * 2 * LANES), BF16),
                   jax.ShapeDtypeStruct((n, HEADS * 2 * LANES), BF16),
                   jax.ShapeDtypeStruct((n, HEADS * VDIM), BF16)],
        compiler_params=_params(("parallel",)),
        name="mla_in",
    )(x2d, g0, win, qn, wuq, kvn, wukv, cos_t, sin_t)


def _attn_kernel(q_ref, k_ref, v_ref, g_ref, x_ref, wout_ref, g1_ref, h_ref, hn_ref, o_ref):
    for h in range(HEADS):
        qh = q_ref[0, :, 2 * LANES * h:2 * LANES * (h + 1)]
        kh = k_ref[0, :, 2 * LANES * h:2 * LANES * (h + 1)]
        s = lax.dot_general(qh, kh, (((1,), (1,)), ((), ())), preferred_element_type=F32)
        m = jnp.max(s, axis=-1, keepdims=True)
        p = jnp.exp2((s - m).astype(BF16))
        ov = _dot(p, v_ref[0, :, 2 * LANES * h:2 * LANES * (h + 1)])
        o_ref[:, VDIM * h:VDIM * (h + 1)] = ov[:, :VDIM] / ov[:, VDIM:VDIM + 1]
    og = (o_ref[...] * g_ref[0].astype(F32)).astype(BF16)
    h1 = x_ref[0] + _dot(og, wout_ref[...])
    h_ref[0] = h1
    hn_ref[0] = _rms(h1, g1_ref[...]).astype(BF16)


def _attention(q, k, v, g, x, wout, g1, tq):
    b, seq, d = x.shape
    qspec = lambda w: pl.BlockSpec((1, tq, w), lambda i, j: (i, j, 0))
    kspec = lambda w: pl.BlockSpec((1, seq, w), lambda i, j: (i, 0, 0))
    full = lambda a: pl.BlockSpec(a.shape, lambda i, j: (0,) * a.ndim)
    return pl.pallas_call(
        _attn_kernel,
        grid=(b, seq // tq),
        in_specs=[qspec(q.shape[-1]), kspec(k.shape[-1]), kspec(v.shape[-1]), qspec(g.shape[-1]),
                  qspec(d), full(wout), full(g1)],
        out_specs=[qspec(d), qspec(d)],
        out_shape=[jax.ShapeDtypeStruct((b, seq, d), F32), jax.ShapeDtypeStruct((b, seq, d), BF16)],
        scratch_shapes=[pltpu.VMEM((tq, HEADS * VDIM), F32)],
        compiler_params=_params(("parallel", "arbitrary")),
        name="mla_attn",
    )(q, k, v, g, x, wout, g1)


def _hyena_in_kernel(hn_ref, w_ref, cw_ref, o_ref, *, rc):
    seq, tn = hn_ref.shape[1], w_ref.shape[1]
    halo = 16
    cc = tn // 3 * 2
    w0, w1, w2, cb = cw_ref[0:1, :cc], cw_ref[1:2, :cc], cw_ref[2:3, :cc], cw_ref[3:4, :cc]
    rows = lax.broadcasted_iota(jnp.int32, (halo, cc), 0)
    nchunk = seq // rc
    bounds = [(max(rc * r - halo, 0), min(rc * (r + 1) + halo, seq)) for r in range(nchunk)]
    project = lambda r: _dot(hn_ref[0, bounds[r][0]:bounds[r][1], :], w_ref[...])
    nxt_proj = project(0)
    for r in range(nchunk):
        lo, hi = bounds[r]
        n = hi - lo
        full = nxt_proj
        if r + 1 < nchunk:
            nxt_proj = project(r + 1)
        first = rc * r - lo
        o_ref[0, rc * r:rc * (r + 1), cc:] = full[first:first + rc, cc:].astype(BF16)
        proj = full[:, :cc]
        prev = pltpu.roll(proj, 1, axis=0)
        nxt = pltpu.roll(proj, n - 1, axis=0)
        out = cb + w0 * prev + w1 * proj + w2 * nxt
        o_ref[0, rc * r:rc * (r + 1), :cc] = out[first:first + rc].astype(BF16)
        if r == 0:
            top = cb + w0 * jnp.where(rows == 0, 0.0, prev[:halo]) + w1 * proj[:halo] + w2 * nxt[:halo]
            o_ref[0, :halo, :cc] = top.astype(BF16)
        if r == nchunk - 1:
            bot = (cb + w0 * prev[n - halo:] + w1 * proj[n - halo:]
                   + w2 * jnp.where(rows == halo - 1, 0.0, nxt[n - halo:]))
            o_ref[0, seq - halo:, :cc] = bot.astype(BF16)


def _hyena_in(hn, w, cw, tn, rc):
    b, seq, d = hn.shape
    nout = w.shape[1]
    return pl.pallas_call(
        functools.partial(_hyena_in_kernel, rc=rc),
        grid=(b, nout // tn),
        in_specs=[pl.BlockSpec((1, seq, d), lambda i, j: (i, 0, 0)),
                  pl.BlockSpec((d, tn), lambda i, j: (0, j)),
                  pl.BlockSpec((SUBLANES, tn), lambda i, j: (0, j))],
        out_specs=pl.BlockSpec((1, seq, tn), lambda i, j: (i, 0, j)),
        out_shape=jax.ShapeDtypeStruct((b, seq, nout), BF16),
        compiler_params=_params(("parallel", "arbitrary")),
        name="hyena_in",
    )(hn, w, cw)


def _bf16_bits(x):
    bits = lax.bitcast_convert_type(x, jnp.uint32)
    return (bits + jnp.uint32(0x7FFF) + ((bits >> 16) & jnp.uint32(1))) >> 16


def _dot_split(a, b):
    a_hi, b_hi = a.astype(BF16), b.astype(BF16)
    a_lo = (a - a_hi.astype(F32)).astype(BF16)
    b_lo = (b - b_hi.astype(F32)).astype(BF16)
    return _dot(a_hi, b_hi) + _dot(a_hi, b_lo) + _dot(a_lo, b_hi)


def _filter_kernel(z_ref, w1_ref, b1_ref, w2_ref, b2_ref, w3_ref, b3_ref, fr_ref, w4_ref, t_ref, dl_ref,
                   kf_ref, a_ref):
    seq = a_ref.shape[1] // 2
    hi = lax.Precision.HIGHEST

    @pl.when((pl.program_id(0) == 0) & (pl.program_id(1) == 0))
    def _():
        fr = fr_ref[...]
        a = jnp.sin(fr * (jnp.dot(w1_ref[...], z_ref[...], precision=hi, preferred_element_type=F32)
                          + b1_ref[...]))
        a = jnp.sin(fr * (jnp.dot(w2_ref[...], a, precision=hi, preferred_element_type=F32) + b2_ref[...]))
        a = jnp.sin(fr * (jnp.dot(w3_ref[...], a, precision=hi, preferred_element_type=F32) + b3_ref[...]))
        a_ref[...] = a

    bwd = _dot_split(w4_ref[0, 1], a_ref[:, :seq])
    fwd = _dot_split(w4_ref[0, 0], a_ref[:, seq:])
    decay = jnp.exp(-(dl_ref[...] * t_ref[...]))
    kf = jnp.concatenate([bwd, fwd], axis=1) * decay
    bits = _bf16_bits(kf)
    kf_ref[0] = bits | (pltpu.roll(bits, 1, axis=1) << 16)


def _filters(zcat, w1t, b1, w2t, b2, w3t, b3, fr, w4t, tcat, deltas, tc):
    _, _, width, hid = w4t.shape
    two_l = zcat.shape[1]
    full = lambda a: pl.BlockSpec(a.shape, lambda n, i: (0,) * a.ndim)
    return pl.pallas_call(
        _filter_kernel,
        grid=(2, width // tc),
        in_specs=[full(zcat), full(w1t), full(b1), full(w2t), full(b2), full(w3t), full(b3), full(fr),
                  pl.BlockSpec((1, 2, tc, hid), lambda n, i: (n, 0, i, 0)),
                  full(tcat), pl.BlockSpec((tc, 1), lambda n, i: (i, 0))],
        out_specs=pl.BlockSpec((1, tc, two_l), lambda n, i: (n, i, 0)),
        out_shape=jax.ShapeDtypeStruct((2, width, two_l), jnp.uint32),
        scratch_shapes=[pltpu.VMEM((hid, two_l), F32)],
        compiler_params=_params(("arbitrary", "arbitrary")),
        name="hyena_filters",
    )(zcat, w1t, b1, w2t, b2, w3t, b3, fr, w4t, tcat, deltas)


def _conv_kernel(x1_ref, x2_ref, v_ref, kf1_ref, kf2_ref, kfn_ref, par_ref, o_ref,
                 lhs_ref, rhs1_ref, rhs2_ref, z1_ref, vc_ref, *, ct):
    nj, nb = v_ref.shape[1], v_ref.shape[2]
    ni = nj // 2
    ntile = kf1_ref.shape[1]
    ng = ntile - 2
    pad = 2 * ni - 2
    rows_per_word_tile = 2 * SUBLANES

    @pl.when(pl.program_id(0) == 0)
    def _():
        lhs_ref[...] = jnp.zeros(lhs_ref.shape, lhs_ref.dtype)

    diff = (lax.broadcasted_iota(jnp.int32, (SUBLANES, LANES), 1)
            - 2 * lax.broadcasted_iota(jnp.int32, (SUBLANES, LANES), 0))
    nroll = LANES // rows_per_word_tile

    lane = lax.broadcasted_iota(jnp.int32, (nb, LANES), 1)
    zero_slab = jnp.zeros((nb, LANES), F32)

    def short_conv_v(c, j):
        w0, w1, w2, cb = (par_ref[c, k:k + 1, :] for k in (2, 3, 4, 5))
        x0 = v_ref[c, j].astype(F32)
        xm = v_ref[c, j - 1].astype(F32) if j > 0 else zero_slab
        xp = v_ref[c, j + 1].astype(F32) if j < nj - 1 else zero_slab
        prev = pltpu.roll(jnp.where(lane == LANES - 1, xm, x0), 1, axis=1)
        nxt = pltpu.roll(jnp.where(lane == 0, xp, x0), LANES - 1, axis=1)
        return cb + w0 * prev + w1 * x0 + w2 * nxt

    def store_lhs(s, j, slab):
        for i in range(ni):
            jj = j + pad - 2 * i
            lhs_ref[s, nb * i:nb * (i + 1), LANES * jj:LANES * (jj + 1)] = slab

    def build_rhs(kf_ref, c, rhs_ref):
        prev = None
        for x in range(ntile - 1, -1, -1):
            tile = jnp.broadcast_to(kf_ref[c, x:x + 1, :], (SUBLANES, LANES))
            cur = [pltpu.roll(tile, rows_per_word_tile * e, axis=1, stride=2, stride_axis=0)
                   for e in range(nroll)]
            if prev is not None:
                g = ng - x
                for e in range(nroll):
                    words = jnp.where(diff < rows_per_word_tile * e, cur[e], prev[e])
                    blk = pltpu.bitcast(words, BF16)
                    r = LANES * g + rows_per_word_tile * e
                    if g < ng:
                        rhs_ref[r:r + rows_per_word_tile, LANES:2 * LANES] = blk
                    if g >= 1:
                        rhs_ref[r - LANES:r - LANES + rows_per_word_tile, 0:LANES] = blk
            prev = cur

    def piece(y, j):
        i, half = divmod(j, 2)
        return y[nb * i:nb * (i + 1), LANES * half:LANES * (half + 1)]

    nslot = lhs_ref.shape[0]
    slots = tuple(range(nslot))
    ngroup = ct // nslot

    @pl.when(pl.program_id(0) == 0)
    def _():
        for s in slots:
            build_rhs(kf1_ref, s, rhs1_ref.at[s])

    def channel_group(p, last):
        cs = [nslot * p + s for s in slots]
        bias1 = [par_ref[c, 0:1, :] for c in cs]
        bias2 = [par_ref[c, 1:2, :] for c in cs]
        for s, c in zip(slots, cs):
            for j in range(nj):
                vc = short_conv_v(c, j)
                vc_ref[s, j] = vc
                store_lhs(s, j, vc.astype(BF16))
        y1 = [_dot(lhs_ref[s], rhs1_ref[s]) for s in slots]
        for s, c in zip(slots, cs):
            build_rhs(kf2_ref, c, rhs2_ref.at[s])
        for s, c in zip(slots, cs):
            for j in range(nj):
                z1 = x1_ref[c, j].astype(F32) * (piece(y1[s], j) + bias1[s] * vc_ref[s, j])
                z1_ref[s, j] = z1
                store_lhs(s, j, z1.astype(BF16))
        y2 = [_dot(lhs_ref[s], rhs2_ref[s]) for s in slots]
        for s, c in zip(slots, cs):
            if last:
                build_rhs(kfn_ref, s, rhs1_ref.at[s])
            else:
                build_rhs(kf1_ref, c + nslot, rhs1_ref.at[s])
        for s, c in zip(slots, cs):
            for j in range(nj):
                z2 = x2_ref[c, j].astype(F32) * (piece(y2[s], j) + bias2[s] * z1_ref[s, j])
                o_ref[c, j] = z2.astype(BF16)

    def body(p, carry):
        channel_group(p, last=False)
        return carry

    lax.fori_loop(0, ngroup - 1, body, 0)
    channel_group(ngroup - 1, last=True)


def _long_conv(ut, kf, par, first_channel, ct):
    _, nj, nb, _ = ut.shape
    wp = ut.shape[0] // 3
    seq = nj * LANES
    kdim = 2 * seq - MXU_N
    nblk = wp // ct
    base = first_channel // ct
    ntile = kf.shape[2]
    uspec = lambda o: pl.BlockSpec((ct, nj, nb, LANES), lambda i: (i + o * nblk, 0, 0, 0))
    kspec = lambda n: pl.BlockSpec((None, ct, ntile, LANES), lambda i: (n, i + base, 0, 0))
    per = ct // CONV_SLOTS
    last_blk = kf.shape[1] // CONV_SLOTS - 1
    nspec = pl.BlockSpec((None, CONV_SLOTS, ntile, LANES),
                         lambda i: (0, jnp.minimum((i + base + 1) * per, last_blk), 0, 0))
    return pl.pallas_call(
        functools.partial(_conv_kernel, ct=ct),
        grid=(nblk,),
        in_specs=[uspec(0), uspec(1), uspec(2), kspec(0), kspec(1), nspec,
                  pl.BlockSpec((ct, SUBLANES, LANES), lambda i: (i + base, 0, 0))],
        out_specs=pl.BlockSpec((ct, nj, nb, LANES), lambda i: (i, 0, 0, 0)),
        out_shape=jax.ShapeDtypeStruct((wp, nj, nb, LANES), BF16),
        scratch_shapes=[pltpu.VMEM((CONV_SLOTS, nj // 2 * nb, kdim), BF16),
                        pltpu.VMEM((CONV_SLOTS, kdim, MXU_N), BF16),
                        pltpu.VMEM((CONV_SLOTS, kdim, MXU_N), BF16),
                        pltpu.VMEM((CONV_SLOTS, nj, nb, LANES), F32),
                        pltpu.VMEM((CONV_SLOTS, nj, nb, LANES), F32)],
        compiler_params=_params(("arbitrary",)),
        name="hyena_long_conv",
    )(ut, ut, ut, kf, kf, kf, par)


def _out_kernel(*refs):
    *z_refs, hn_ref, h_ref, wg_ref, w_ref, gf_ref, o_ref = refs
    gate = _dot(hn_ref[...], wg_ref[...])
    z = jnp.concatenate([r[...] for r in z_refs], axis=1).astype(F32)
    zg = (z * (gate * jax.nn.sigmoid(gate))).astype(BF16)
    h2 = h_ref[...] + _dot(zg, w_ref[...])
    o_ref[...] = _rms(h2, gf_ref[...])


def _hyena_out(z_parts, hn2d, h2d, wg, w, gf, tm):
    n, d = h2d.shape
    row = lambda w_: pl.BlockSpec((tm, w_), lambda i: (i, 0))
    full = lambda a: pl.BlockSpec(a.shape, lambda i: (0,) * a.ndim)
    return pl.pallas_call(
        _out_kernel,
        grid=(n // tm,),
        in_specs=[row(z.shape[1]) for z in z_parts] + [row(d), row(d), full(wg), full(w), full(gf)],
        out_specs=row(d),
        out_shape=jax.ShapeDtypeStruct((n, d), F32),
        compiler_params=_params(("parallel",)),
        name="hyena_out",
    )(*z_parts, hn2d, h2d, wg, w, gf)


def _rope_tables(seq):
    inv = 1.0 / (ROPE_THETA ** (jnp.arange(0, ROPE, 2, dtype=F32) / ROPE))
    ang = jnp.arange(seq, dtype=F32)[:, None] * inv[None, :]
    cos, sin = jnp.cos(ang), jnp.sin(ang)
    zero = jnp.zeros((seq, LANES - ROPE), F32)
    return (jnp.concatenate([cos, cos, zero], axis=1), jnp.concatenate([-sin, sin, zero], axis=1))


def _filter_tables(seq, width):
    t = np.linspace(0.0, 1.0, seq)[:, None]
    w = 2.0 * math.pi * np.arange(seq) / seq
    bands = np.linspace(1e-4, POS_BANDS - 1, POS_BANDS)
    fw = w[:, None] * bands[None, :]
    z = np.concatenate([t, np.cos(fw), -np.sin(fw)], axis=-1)
    deltas = np.abs(np.linspace(MIN_DECAY, MAX_DECAY, width))
    idx = np.abs(np.arange(2 * seq) - seq)
    idx[0] = 0
    zcat = np.pad(z[idx].T, ((0, FILT_HIDDEN - POS_EMB), (0, 0)))
    tcat = t[idx].T.copy()
    tcat[:, 0] = 1e4
    return zcat.astype(np.float32), tcat.astype(np.float32), deltas[:, None].astype(np.float32)


def kernel(x, l0_norm, l0_w_in, l0_q_norm, l0_w_uq, l0_kv_norm, l0_w_ukv, l0_w_out, l1_norm, l1_w_in,
           l1_conv_w, l1_conv_b, l1_filt_w1, l1_filt_b1, l1_filt_w2, l1_filt_b2, l1_filt_w3, l1_filt_b3,
           l1_filt_w4, l1_filt_freq, l1_filt_bias, l1_w_out, final_norm):
    b, seq, d = x.shape
    width = l1_w_out.shape[0]
    n = b * seq
    tm = min(TOKEN_TILE, seq)
    tq = min(QUERY_TILE, seq)
    row = lambda a: a.reshape(1, -1).astype(F32)
    col = lambda a: a.reshape(-1, 1).astype(F32)

    s2, s3 = Q_RANK + KV_RANK, Q_RANK + KV_RANK + ROPE
    kpe_w = l0_w_in[:, s2:s3]
    win = jnp.concatenate([l0_w_in[:, :s2], kpe_w, kpe_w, l0_w_in[:, s3:]], axis=1).astype(BF16)
    wuq3 = l0_w_uq.reshape(Q_RANK, HEADS, QK_DIM)
    wuq = jnp.concatenate([wuq3, wuq3[:, :, NOPE:]], axis=2).reshape(Q_RANK, -1).astype(BF16)
    wukv3 = l0_w_ukv.reshape(KV_RANK, HEADS, NOPE + VDIM)
    wukv = jnp.concatenate([wukv3[:, :, :NOPE].reshape(KV_RANK, -1),
                            wukv3[:, :, NOPE:].reshape(KV_RANK, -1)], axis=1).astype(BF16)
    cos_t, sin_t = _rope_tables(seq)

    q, k, v, g = _mla_in(x.reshape(n, d), row(l0_norm), win, row(l0_q_norm), wuq, row(l0_kv_norm),
                         wukv, cos_t, sin_t, seq, tm)
    h1, h1n = _attention(q.reshape(b, seq, -1), k.reshape(b, seq, -1), v.reshape(b, seq, -1),
                         g.reshape(b, seq, -1), x, l0_w_out.astype(BF16), row(l1_norm), tq)

    nu = 3 * width
    nj = seq // LANES
    cw = jnp.concatenate([l1_conv_w, l1_conv_b[None, :], jnp.zeros((SUBLANES - 4, nu), F32)],
                         axis=0).astype(F32)
    w_in1 = l1_w_in.astype(BF16)
    parts = LAYER1_PARTS
    wp = width // parts
    uts = []
    for part in range(parts):
        take = lambda a: jnp.concatenate([a[:, grp * width + part * wp:grp * width + (part + 1) * wp]
                                          for grp in range(3)], axis=1)
        u = _hyena_in(h1n, take(w_in1), take(cw), 3 * wp, min(CONV_ROW_CHUNK, seq))
        uts.append(u.reshape(b, nj, LANES, 3 * wp).transpose(3, 1, 0, 2))

    zcat, tcat, deltas = _filter_tables(seq, width)
    w4t = l1_filt_w4.T.reshape(2, 2, width, FILT_HIDDEN).astype(F32)
    w1t = jnp.pad(l1_filt_w1.T.astype(F32), ((0, 0), (0, FILT_HIDDEN - POS_EMB)))
    kf = _filters(zcat, w1t, col(l1_filt_b1), l1_filt_w2.T.astype(F32), col(l1_filt_b2),
                  l1_filt_w3.T.astype(F32), col(l1_filt_b3), col(l1_filt_freq), w4t, tcat, deltas,
                  min(FILTER_CHANNELS, width))
    kf = kf.reshape(2, width, 2 * seq // LANES, LANES)

    par = jnp.concatenate([l1_filt_bias.astype(F32), cw[:4, 2 * width:], jnp.zeros((2, width), F32)], axis=0)
    par = jnp.broadcast_to(par.T[:, :, None], (width, SUBLANES, LANES))

    z_parts = []
    for part in range(parts):
        z2t = _long_conv(uts[part], kf, par, part * wp, CONV_CHANNELS)
        z_parts.append(z2t.transpose(2, 1, 3, 0).reshape(n, wp))
    out = _hyena_out(z_parts, h1n.reshape(n, d), h1.reshape(n, d), w_in1[:, nu:], l1_w_out.astype(BF16),
                     row(final_norm), min(OUT_TILE, seq))
    return out.reshape(b, seq, d)
```
